```python
import math
import jax, jax.numpy as jnp
from jax import lax
import numpy as np

D_MODEL = 1024
BATCH = 32
SEQ = 2048
DEPTH = 4
DEC_BATCH = 1
DEC_SEQ = 16384
PAST_LEN = 128

HEAD_DIM = 64
GROUP_HEADS = 4
GROUP_WIDTH = GROUP_HEADS * HEAD_DIM
N_MIXERS = 4
MIX_WIDTH = N_MIXERS * GROUP_WIDTH
A_KV_HEADS = 2
B_KV_HEADS = 2
GRID_W = 64
ROPE_THETA = 10000.0
Q_BLOCK = 128
B_RADIUS = 128
C_BRANCHES = ((128, 1), (512, 4), (2048, 16))
NA_ROWS = 8
NA_COLS = 16
NA_COL_BLOCK = 16
NA_COL_BAND = 32
MEM_LEN = 256
X_HEADS = 4
X_HEAD_DIM = D_MODEL // X_HEADS
D_FF = 2816
CONV_W = 3
EPS = 1e-6
NEG = -1e30

IN_SIZES = (GROUP_WIDTH, A_KV_HEADS * HEAD_DIM, A_KV_HEADS * HEAD_DIM,
            GROUP_WIDTH, B_KV_HEADS * HEAD_DIM, B_KV_HEADS * HEAD_DIM,
            GROUP_WIDTH, GROUP_WIDTH, GROUP_WIDTH,
            GROUP_WIDTH, GROUP_WIDTH, GROUP_WIDTH)
IN_WIDTH = sum(IN_SIZES)
IN_SPLITS = tuple(int(s) for s in np.cumsum(IN_SIZES)[:-1])

kernel_name = "hybrid_parallel_group_encoder"


def rms_norm(x, g):
    xf = x.astype(jnp.float32)
    y = xf * lax.rsqrt(jnp.mean(xf * xf, axis=-1, keepdims=True) + EPS)
    return (y * g.astype(jnp.float32)).astype(x.dtype)


def rope(x, pos):
    half = x.shape[-1] // 2
    freqs = ROPE_THETA ** (-jnp.arange(half, dtype=jnp.float32) / half)
    ang = pos[:, None] * freqs[None, :]
    cos = jnp.cos(ang)[None, :, None, :]
    sin = jnp.sin(ang)[None, :, None, :]
    xf = x.astype(jnp.float32)
    x1, x2 = xf[..., :half], xf[..., half:]
    return jnp.concatenate([x1 * cos - x2 * sin, x2 * cos + x1 * sin], axis=-1).astype(x.dtype)


def banded_attention(q, k, v, radius, block, sink=None):
    B, L, H, D = q.shape
    KVH = k.shape[2]
    G = H // KVH
    nb = -(-L // block)
    Lp = nb * block
    width = block + 2 * radius
    qp = jnp.pad(q, ((0, 0), (0, Lp - L), (0, 0), (0, 0)))
    pad_kv = ((0, 0), (radius, Lp - L + radius), (0, 0), (0, 0))
    kp = jnp.pad(k, pad_kv)
    vp = jnp.pad(v, pad_kv)
    idx = jnp.arange(nb)[:, None] * block + jnp.arange(width)[None, :]
    kb = kp[:, idx]
    vb = vp[:, idx]
    qb = qp.reshape(B, nb, block, KVH, G, D)
    s = jnp.einsum('bnqkgd,bnwkd->bnkgqw', qb, kb).astype(jnp.float32) * (D ** -0.5)
    qpos = jnp.arange(nb)[:, None] * block + jnp.arange(block)[None, :]
    kpos = idx - radius
    dist = jnp.abs(kpos[:, None, :] - qpos[:, :, None])
    valid = (dist <= radius) & (kpos[:, None, :] >= 0) & (kpos[:, None, :] < L)
    s = jnp.where(valid[None, :, None, None], s, NEG)
    lse = jax.nn.logsumexp(s, axis=-1)
    if sink is not None:
        lse = jnp.logaddexp(lse, sink.astype(jnp.float32).reshape(KVH, G)[None, None, :, :, None])
    p = jnp.exp(s - lse[..., None])
    o = jnp.einsum('bnkgqw,bnwkd->bnqkgd', p.astype(v.dtype), vb).reshape(B, Lp, H, D)[:, :L]
    lse = lse.transpose(0, 1, 4, 2, 3).reshape(B, Lp, H)[:, :L]
    return o, lse


def global_axial_attention(q, k, v, q_g, k_g):
    B, T, H, D = q.shape
    KVH = k.shape[2]
    G = H // KVH
    t = jnp.arange(T)
    row = (t // GRID_W).astype(jnp.float32)
    col = (t % GRID_W).astype(jnp.float32)
    half = D // 2
    q = rms_norm(q, q_g)
    k = rms_norm(k, k_g)
    q = jnp.concatenate([rope(q[..., :half], row), rope(q[..., half:], col)], axis=-1)
    k = jnp.concatenate([rope(k[..., :half], row), rope(k[..., half:], col)], axis=-1)
    nb = T // Q_BLOCK
    qb = q.reshape(B, nb, Q_BLOCK, KVH, G, D).transpose(1, 0, 2, 3, 4, 5)
    scale = D ** -0.5

    def one_block(qi):
        s = jnp.einsum('bqkgd,bskd->bkgqs', qi, k).astype(jnp.float32) * scale
        p = jax.nn.softmax(s, axis=-1)
        return jnp.einsum('bkgqs,bskd->bqkgd', p.astype(v.dtype), v)

    out = lax.map(one_block, qb)
    return out.transpose(1, 0, 2, 3, 4, 5).reshape(B, T, H, D)


def window_sink_attention(q, k, v, sink):
    pos = jnp.arange(q.shape[1], dtype=jnp.float32)
    out, _ = banded_attention(rope(q, pos), rope(k, pos), v, B_RADIUS, Q_BLOCK, sink)
    return out


def _to_sub(x, d):
    B, T, H, D = x.shape
    return x.reshape(B, T // d, d, H, D).transpose(0, 2, 1, 3, 4).reshape(B * d, T // d, H, D)


def dilated_attention(q, k, v):
    B, T, H, D = q.shape
    pos = jnp.arange(T, dtype=jnp.float32)
    q = rope(q, pos)
    k = rope(k, pos)
    outs, lses = [], []
    for window, d in C_BRANCHES:
        radius = window // (2 * d)
        o, l = banded_attention(_to_sub(q, d), _to_sub(k, d), _to_sub(v, d), radius, radius)
        outs.append(o.reshape(B, d, T // d, H, D).transpose(0, 2, 1, 3, 4).reshape(B, T, H, D))
        lses.append(l.reshape(B, d, T // d, H).transpose(0, 2, 1, 3).reshape(B, T, H))
    w = jax.nn.softmax(jnp.stack(lses, axis=0), axis=0)
    return jnp.einsum('nbth,nbthd->bthd', w.astype(q.dtype), jnp.stack(outs, axis=0))


def neighbourhood_attention(q, k, v, rpb):
    B, T, H, D = q.shape
    rows = T // GRID_W
    kh = min(NA_ROWS, rows)
    ncb = GRID_W // NA_COL_BLOCK
    r = jnp.arange(rows)
    row_start = jnp.clip(r - kh // 2, 0, rows - kh)
    row_idx = row_start[:, None] + jnp.arange(kh)[None, :]
    band_start = jnp.clip(jnp.arange(ncb) * NA_COL_BLOCK - NA_COLS // 2, 0, GRID_W - NA_COL_BAND)
    col_idx = band_start[:, None] + jnp.arange(NA_COL_BAND)[None, :]
    kg = k.reshape(B, rows, GRID_W, H, D)
    vg = v.reshape(B, rows, GRID_W, H, D)
    ri = row_idx[:, None, :, None]
    ci = col_idx[None, :, None, :]
    kb = kg[:, ri, ci]
    vb = vg[:, ri, ci]
    qb = q.reshape(B, rows, ncb, NA_COL_BLOCK, H, D)
    s = jnp.einsum('brcjhd,brcmnhd->bhrcjmn', qb, kb).astype(jnp.float32) * (D ** -0.5)
    qcol = jnp.arange(GRID_W).reshape(ncb, NA_COL_BLOCK)
    qcol_start = jnp.clip(qcol - NA_COLS // 2, 0, GRID_W - NA_COLS)
    rel = col_idx[:, None, :] - qcol_start[:, :, None]
    col_valid = (rel >= 0) & (rel < NA_COLS)
    dc = jnp.clip(col_idx[:, None, :] - qcol[:, :, None], -(NA_COLS - 1), NA_COLS - 1) + (NA_COLS - 1)
    dr = row_idx - r[:, None] + (NA_ROWS - 1)
    bias = rpb[:, dr[:, None, None, :, None], dc[None, :, :, None, :]]
    s = s + bias[None].astype(jnp.float32)
    s = jnp.where(col_valid[None, None, None, :, :, None, :], s, NEG)
    p = jax.nn.softmax(s.reshape(s.shape[:-2] + (kh * NA_COL_BAND,)), axis=-1).reshape(s.shape)
    o = jnp.einsum('bhrcjmn,brcmnhd->brcjhd', p.astype(v.dtype), vb)
    return o.reshape(B, T, H, D)


def token_mixer(h, w_in, a_q_norm_g, a_k_norm_g, b_sink, d_rpb, grp_norm_g, w_out):
    B, T, _ = h.shape
    proj = h @ w_in
    aq, ak, av, bq, bk, bv, cq, ck, cv, dq, dk, dv = jnp.split(proj, IN_SPLITS, axis=-1)
    heads = lambda z: z.reshape(B, T, -1, HEAD_DIM)
    ya = global_axial_attention(heads(aq), heads(ak), heads(av), a_q_norm_g, a_k_norm_g)
    yb = window_sink_attention(heads(bq), heads(bk), heads(bv), b_sink)
    yc = dilated_attention(heads(cq), heads(ck), heads(cv))
    yd = neighbourhood_attention(heads(dq), heads(dk), heads(dv), d_rpb)
    y = jnp.stack([z.reshape(B, T, GROUP_WIDTH) for z in (ya, yb, yc, yd)], axis=2)
    y = rms_norm(y, grp_norm_g)
    return y.reshape(B, T, MIX_WIDTH) @ w_out


def memory_cross_attention(h, mem, mem_g, w_xq, w_xkv, w_xo):
    B, T, _ = h.shape
    M = mem.shape[1]
    m = rms_norm(mem, mem_g)
    q = (h @ w_xq).reshape(B, T, X_HEADS, X_HEAD_DIM)
    kv = (m @ w_xkv).reshape(B, M, 2, X_HEADS, X_HEAD_DIM)
    k, v = kv[:, :, 0], kv[:, :, 1]
    s = jnp.einsum('bthd,bmhd->bhtm', q, k).astype(jnp.float32) * (X_HEAD_DIM ** -0.5)
    p = jax.nn.softmax(s, axis=-1)
    o = jnp.einsum('bhtm,bmhd->bthd', p.astype(v.dtype), v).reshape(B, T, D_MODEL)
    return o @ w_xo


def conv_glu_ffn(h, w_up, conv_w, conv_b, w_down):
    u = h @ w_up
    gate, val = jnp.split(u, 2, axis=-1)
    gate = lax.conv_general_dilated(gate, conv_w[:, None, :], window_strides=(1,),
                                    padding=((CONV_W // 2, CONV_W // 2),),
                                    dimension_numbers=('NWC', 'WIO', 'NWC'),
                                    feature_group_count=D_FF) + conv_b
    return (jax.nn.gelu(gate, approximate=False) * val) @ w_down


def encoder_trunk(x, mem, norm_mix_g, w_in, a_q_norm_g, a_k_norm_g, b_sink, d_rpb, grp_norm_g, w_out,
                  norm_x_g, norm_mem_g, w_xq, w_xkv, w_xo, norm_ffn_g, w_up, conv_w, conv_b, w_down,
                  final_norm_g):
    for l in range(DEPTH):
        x = x + token_mixer(rms_norm(x, norm_mix_g[l]), w_in[l], a_q_norm_g[l], a_k_norm_g[l],
                            b_sink[l], d_rpb[l], grp_norm_g[l], w_out[l])
        x = x + memory_cross_attention(rms_norm(x, norm_x_g[l]), mem, norm_mem_g[l],
                                       w_xq[l], w_xkv[l], w_xo[l])
        x = x + conv_glu_ffn(rms_norm(x, norm_ffn_g[l]), w_up[l], conv_w[l], conv_b[l], w_down[l])
    return rms_norm(x, final_norm_g)


def setup_inputs(seed: int = 0) -> dict:
    key = jax.random.key(seed)
    ks = jax.random.split(key, 24)
    f32 = jnp.float32

    def w(k, shape, fan_in):
        return jax.random.normal(k, shape, f32) * (fan_in ** -0.5)

    def gain(k, shape):
        return 1.0 + 0.02 * jax.random.normal(k, shape, f32)

    return {
        'x_prompt': jax.random.normal(ks[0], (BATCH, SEQ, D_MODEL), f32),
        'x_sample': jax.random.normal(ks[1], (DEC_BATCH, DEC_SEQ, D_MODEL), f32),
        'mem_prompt': jax.random.normal(ks[2], (BATCH, MEM_LEN, D_MODEL), f32),
        'mem_sample': jax.random.normal(ks[3], (DEC_BATCH, MEM_LEN, D_MODEL), f32),
        'norm_mix_g': gain(ks[4], (DEPTH, D_MODEL)),
        'w_in': w(ks[5], (DEPTH, D_MODEL, IN_WIDTH), D_MODEL),
        'a_q_norm_g': gain(ks[6], (DEPTH, HEAD_DIM)),
        'a_k_norm_g': gain(ks[7], (DEPTH, HEAD_DIM)),
        'b_sink': 0.5 * jax.random.normal(ks[8], (DEPTH, GROUP_HEADS), f32),
        'd_rpb': 0.1 * jax.random.normal(ks[9], (DEPTH, GROUP_HEADS, 2 * NA_ROWS - 1, 2 * NA_COLS - 1), f32),
        'grp_norm_g': gain(ks[10], (DEPTH, N_MIXERS, GROUP_WIDTH)),
        'w_out': w(ks[11], (DEPTH, MIX_WIDTH, D_MODEL), MIX_WIDTH),
        'norm_x_g': gain(ks[12], (DEPTH, D_MODEL)),
        'norm_mem_g': gain(ks[13], (DEPTH, D_MODEL)),
        'w_xq': w(ks[14], (DEPTH, D_MODEL, D_MODEL), D_MODEL),
        'w_xkv': w(ks[15], (DEPTH, D_MODEL, 2 * D_MODEL), D_MODEL),
        'w_xo': w(ks[16], (DEPTH, D_MODEL, D_MODEL), D_MODEL),
        'norm_ffn_g': gain(ks[17], (DEPTH, D_MODEL)),
        'w_up': w(ks[18], (DEPTH, D_MODEL, 2 * D_FF), D_MODEL),
        'conv_w': w(ks[19], (DEPTH, CONV_W, D_FF), CONV_W),
        'conv_b': 0.02 * jax.random.normal(ks[20], (DEPTH, D_FF), f32),
        'w_down': w(ks[21], (DEPTH, D_FF, D_MODEL), D_FF),
        'final_norm_g': gain(ks[22], (D_MODEL,)),
    }


def reference(x_prompt, x_sample, mem_prompt, mem_sample, norm_mix_g, w_in, a_q_norm_g, a_k_norm_g,
              b_sink, d_rpb, grp_norm_g, w_out, norm_x_g, norm_mem_g, w_xq, w_xkv, w_xo, norm_ffn_g,
              w_up, conv_w, conv_b, w_down, final_norm_g):
    y_prompt = encoder_trunk(x_prompt, mem_prompt, norm_mix_g, w_in, a_q_norm_g, a_k_norm_g, b_sink, d_rpb,
                             grp_norm_g, w_out, norm_x_g, norm_mem_g, w_xq, w_xkv, w_xo, norm_ffn_g,
                             w_up, conv_w, conv_b, w_down, final_norm_g)
    y_sample = encoder_trunk(x_sample, mem_sample, norm_mix_g, w_in, a_q_norm_g, a_k_norm_g, b_sink, d_rpb,
                             grp_norm_g, w_out, norm_x_g, norm_mem_g, w_xq, w_xkv, w_xo, norm_ffn_g,
                             w_up, conv_w, conv_b, w_down, final_norm_g)
    return (y_prompt, y_sample)
```

```python
import functools

import numpy as np
import jax
import jax.numpy as jnp
from jax import lax
from jax.experimental import pallas as pl
from jax.experimental.pallas import tpu as pltpu

F32 = jnp.float32
BF16 = jnp.bfloat16

D_MODEL = 1024
DEPTH = 4
HEAD_DIM = 64
GROUP_WIDTH = 256
GRID_W = 64
ROPE_THETA = 10000.0
B_RADIUS = 128
C_BRANCHES = ((128, 1), (512, 4), (2048, 16))
NA_ROWS = 8
NA_COLS = 16
MEM_LEN = 256
X_HEADS = 4
X_HEAD_DIM = D_MODEL // X_HEADS
D_FF = 2816
EPS = 1e-6
NEG = -1e30
IN_WIDTH = 2560

LANES = 128
VMEM_LIMIT = 56 * 1024 * 1024
TOKEN_TILE = 512
BAND_TOKENS = 2048
A_TQ = 256
A_TK = 2048
FF_CHUNK = 1408

GQA_HEAD_ORDER = (0, 2, 1, 3)
NT_DIMS = (((1,), (1,)), ((), ()))


def _cparams(*sem):
    return pltpu.CompilerParams(dimension_semantics=sem, vmem_limit_bytes=VMEM_LIMIT)


def _const_spec(shape):
    n = len(shape)
    return pl.BlockSpec(shape, lambda *_: (0,) * n)


def _resident_spec(shape):
    n = len(shape)
    return pl.BlockSpec(shape, lambda *_: (0,) * n, pipeline_mode=pl.Buffered(1))


def _rms_rows(x, g):
    return x * lax.rsqrt(jnp.mean(x * x, axis=-1, keepdims=True) + EPS) * g


def _low_lanes():
    return lax.broadcasted_iota(jnp.int32, (1, LANES), 1) < HEAD_DIM


def _stack_heads(qb, low):
    zero = jnp.zeros_like(qb)
    return jnp.concatenate([jnp.where(low, qb, zero), jnp.where(low, zero, qb)], axis=0)


def _rope(y, c, sa, sb, shift):
    up = pltpu.roll(y, LANES - shift, 1)
    dn = pltpu.roll(y, shift, 1)
    return y * c + up * sa + dn * sb


def _proj_kernel(x_ref, g_ref, w_ref, tab_ref, gq_ref, gk_ref, bd_ref,
                 qa, ka, va, qb, kb, vb, qc, kc, vc, qd, kd, vd):
    h = _rms_rows(x_ref[...], g_ref[...]).astype(BF16)

    def mm(c0):
        return jnp.dot(h, w_ref[:, c0:c0 + 256], preferred_element_type=F32)

    def tab(k):
        return tab_ref[:, k * LANES:(k + 1) * LANES]

    def rope_lin(y):
        return _rope(y, tab(0), tab(1), tab(2), HEAD_DIM // 2)

    def rope_ax(y):
        return _rope(y, tab(3), tab(4), tab(5), HEAD_DIM // 4)

    def qknorm(y, g):
        ss = jnp.dot((y * y).astype(BF16), bd_ref[...], preferred_element_type=F32) * (1.0 / HEAD_DIM)
        return y * lax.rsqrt(ss + EPS) * g

    def halves(y):
        return y[:, :LANES], y[:, LANES:]

    scale = HEAD_DIM ** -0.5

    y0, y1 = halves(mm(0))
    qa[:, :LANES] = rope_ax(qknorm(y0, gq_ref[...])).astype(BF16)
    qa[:, LANES:] = rope_ax(qknorm(y1, gq_ref[...])).astype(BF16)
    y0, y1 = halves(mm(256))
    ka[...] = rope_ax(qknorm(y0, gk_ref[...])).astype(BF16)
    va[...] = y1.astype(BF16)

    y0, y1 = halves(mm(512))
    qb[:, :LANES] = rope_lin(y0 * scale).astype(BF16)
    qb[:, LANES:] = rope_lin(y1 * scale).astype(BF16)
    y0, y1 = halves(mm(768))
    kb[...] = rope_lin(y0).astype(BF16)
    vb[...] = y1.astype(BF16)

    y0, y1 = halves(mm(1024))
    qc[:, :LANES] = rope_lin(y0 * scale).astype(BF16)
    qc[:, LANES:] = rope_lin(y1 * scale).astype(BF16)
    y0, y1 = halves(mm(1280))
    kc[:, :LANES] = rope_lin(y0).astype(BF16)
    kc[:, LANES:] = rope_lin(y1).astype(BF16)
    vc[...] = mm(1536).astype(BF16)

    qd[...] = (mm(1792) * scale).astype(BF16)
    kd[...] = mm(2048).astype(BF16)
    vd[...] = mm(2304).astype(BF16)


def _proj(x, g, w, tab, gq, gk, bd, T):
    n = x.shape[0]
    tm = TOKEN_TILE
    tpb = T // tm
    widths = (256, 128, 128, 256, 128, 128, 256, 256, 256, 256, 256, 256)
    return pl.pallas_call(
        _proj_kernel,
        grid=(n // tm,),
        in_specs=[
            pl.BlockSpec((tm, D_MODEL), lambda i: (i, 0)),
            _const_spec((1, D_MODEL)),
            _resident_spec((D_MODEL, IN_WIDTH)),
            pl.BlockSpec((tm, 6 * LANES), lambda i: (i % tpb, 0)),
            _const_spec((1, LANES)),
            _const_spec((1, LANES)),
            _const_spec((LANES, LANES)),
        ],
        out_specs=[pl.BlockSpec((tm, wd), lambda i: (i, 0)) for wd in widths],
        out_shape=[jax.ShapeDtypeStruct((n, wd), BF16) for wd in widths],
        compiler_params=_cparams("parallel"),
        name="proj",
    )(x, g, w, tab, gq, gk, bd)


def _attn_a_kernel(q_ref, k_ref, v_ref, o_ref, m_sc, l_sc, acc_sc, *, nk):
    j = pl.program_id(2)
    tq = q_ref.shape[0]

    @pl.when(j == 0)
    def _():
        m_sc[...] = jnp.full(m_sc.shape, NEG, F32)
        l_sc[...] = jnp.zeros(l_sc.shape, F32)
        acc_sc[...] = jnp.zeros(acc_sc.shape, F32)

    low = _low_lanes()
    k = k_ref[...]
    v = v_ref[...]
    for jb in range(2):
        cols = slice(jb * LANES, (jb + 1) * LANES)
        q2 = _stack_heads(q_ref[:, cols], low)
        s = lax.dot_general(q2, k, NT_DIMS, preferred_element_type=F32)
        m_prev = m_sc[jb]
        m_new = jnp.maximum(m_prev, jnp.max(s, axis=-1, keepdims=True))
        alpha = jnp.exp(m_prev - m_new)
        p = jnp.exp(s - m_new)
        l_sc[jb] = alpha * l_sc[jb] + jnp.sum(p, axis=-1, keepdims=True)
        m_sc[jb] = m_new
        pv = jnp.dot(p.astype(BF16), v, preferred_element_type=F32)
        a2 = jnp.where(low, alpha[:tq], alpha[tq:])
        acc_sc[:, cols] = acc_sc[:, cols] * a2 + jnp.where(low, pv[:tq], pv[tq:])

    @pl.when(j == nk - 1)
    def _():
        for jb in range(2):
            cols = slice(jb * LANES, (jb + 1) * LANES)
            l = l_sc[jb]
            o_ref[:, cols] = acc_sc[:, cols] / jnp.where(low, l[:tq], l[tq:])


def _attn_a(q, k, v, B, T):
    n = q.shape[0]
    tq = A_TQ
    tk = min(A_TK, T)
    nq, nk = T // tq, T // tk
    return pl.pallas_call(
        functools.partial(_attn_a_kernel, nk=nk),
        grid=(B, nq, nk),
        in_specs=[
            pl.BlockSpec((tq, 256), lambda b, i, j: (b * nq + i, 0)),
            pl.BlockSpec((tk, LANES), lambda b, i, j: (b * nk + j, 0)),
            pl.BlockSpec((tk, LANES), lambda b, i, j: (b * nk + j, 0)),
        ],
        out_specs=pl.BlockSpec((tq, 256), lambda b, i, j: (b * nq + i, 0)),
        out_shape=jax.ShapeDtypeStruct((n, 256), F32),
        scratch_shapes=[
            pltpu.VMEM((2, 2 * tq, 1), F32),
            pltpu.VMEM((2, 2 * tq, 1), F32),
            pltpu.VMEM((tq, 256), F32),
        ],
        compiler_params=_cparams("parallel", "parallel", "arbitrary"),
        name="attn_a",
    )(q, k, v)


def _band_kernel(*refs, S, W, R, lead, L, kvw, mode):
    if mode == "sink":
        sink_ref, q_ref, k_ref, v_ref, o_ref = refs
    elif mode == "lse":
        q_ref, k_ref, v_ref, o_ref, lse_ref = refs
    else:
        q_ref, k_ref, v_ref, bias_ref, o_ref = refs
    qi = pl.program_id(1)
    G, TQ = q_ref.shape[0], q_ref.shape[1]
    per = TQ // S
    low = _low_lanes()
    row = lax.broadcasted_iota(jnp.int32, (2 * S, W), 0)
    col = lax.broadcasted_iota(jnp.int32, (2 * S, W), 1)
    base = col - jnp.where(row >= S, row - S, row)
    first_half = lax.broadcasted_iota(jnp.int32, (2 * S, 1), 0) < S

    def body(it, carry):
        g = it // per
        i = it % per
        ig = qi * per + i
        start = pl.multiple_of(jnp.clip((ig - lead) * S, 0, L - W), S)
        q0 = pl.multiple_of(i * S, S)
        qs = q_ref[g, pl.ds(q0, S), :]
        ks = k_ref[g, pl.ds(start, W), :]
        vs = v_ref[g, pl.ds(start, W), :]
        if mode == "bias":
            e = ig - start // S
        else:
            valid = jnp.abs(base + (start - ig * S)) <= R
        for jb in range(2):
            cols = slice(jb * LANES, (jb + 1) * LANES)
            kcols = cols if kvw == 256 else slice(0, LANES)
            q2 = _stack_heads(qs[:, cols], low)
            s = lax.dot_general(q2, ks[:, kcols], NT_DIMS, preferred_element_type=F32)
            if mode == "bias":
                s = s + bias_ref[e, jb]
            else:
                s = jnp.where(valid, s, NEG)
            m = jnp.max(s, axis=-1, keepdims=True)
            if mode == "sink":
                sk = jnp.where(first_half, sink_ref[jb, 0], sink_ref[jb, 1])
                m = jnp.maximum(m, sk)
            p = jnp.exp(s - m)
            l = jnp.sum(p, axis=-1, keepdims=True)
            if mode == "sink":
                l = l + jnp.exp(sk - m)
            pv = jnp.dot(p.astype(BF16), vs[:, kcols], preferred_element_type=F32)
            o2 = pv / l
            o_ref[g, pl.ds(q0, S), cols] = jnp.where(low, o2[:S], o2[S:])
            if mode == "lse":
                lse = m + jnp.log(l)
                lse_ref[g, pl.ds(q0, S), cols] = jnp.where(low, lse[:S], lse[S:])
        return carry

    lax.fori_loop(0, G * per, body, 0)


def _band(q, k, v, *, nseq, L, S, W, R, lead, mode, extra=None):
    kvw = k.shape[1]
    W = min(W, L)
    TQ = min(L, BAND_TOKENS)
    G = max(1, BAND_TOKENS // L)
    assert nseq % G == 0 and L % TQ == 0 and TQ % S == 0 and (L - W) % S == 0
    q3 = q.reshape(nseq, L, 256)
    k3 = k.reshape(nseq, L, kvw)
    v3 = v.reshape(nseq, L, kvw)
    qspec = pl.BlockSpec((G, TQ, 256), lambda s, i: (s, i, 0))
    kvspec = pl.BlockSpec((G, L, kvw), lambda s, i: (s, 0, 0))
    in_specs = [qspec, kvspec, kvspec]
    args = [q3, k3, v3]
    out_specs = qspec
    out_shape = jax.ShapeDtypeStruct((nseq, L, 256), F32)
    if mode == "sink":
        in_specs = [pl.BlockSpec(memory_space=pltpu.SMEM)] + in_specs
        args = [extra] + args
    elif mode == "bias":
        in_specs = in_specs + [_const_spec(extra.shape)]
        args = args + [extra]
    else:
        out_specs = [qspec, qspec]
        out_shape = [out_shape, out_shape]
    out = pl.pallas_call(
        functools.partial(_band_kernel, S=S, W=W, R=R, lead=lead, L=L, kvw=kvw, mode=mode),
        grid=(nseq // G, L // TQ),
        in_specs=in_specs,
        out_specs=out_specs,
        out_shape=out_shape,
        compiler_params=_cparams("parallel", "arbitrary"),
        name="band_" + mode,
    )(*args)
    if mode == "lse":
        return out[0].reshape(nseq * L, 256), out[1].reshape(nseq * L, 256)
    return out.reshape(nseq * L, 256)


def _to_sub(z, B, T, d):
    w = z.shape[1]
    return z.reshape(B, T // d, d, w).transpose(0, 2, 1, 3).reshape(B * T, w)


def _from_sub(z, B, T, d):
    w = z.shape[1]
    return z.reshape(B, d, T // d, w).transpose(0, 2, 1, 3).reshape(B * T, w)


def _mixout_kernel(x_ref, ya_ref, yb_ref, c1o, c1l, c2o, c2l, c3o, c3l, yd_ref,
                   gg_ref, wo_ref, gx_ref, wq_ref, xo_ref, qx_ref):
    def gn(y, gi):
        return _rms_rows(y, gg_ref[gi:gi + 1, :]).astype(BF16)

    l1, l2, l3 = c1l[...], c2l[...], c3l[...]
    mx = jnp.maximum(jnp.maximum(l1, l2), l3)
    e1, e2, e3 = jnp.exp(l1 - mx), jnp.exp(l2 - mx), jnp.exp(l3 - mx)
    yc = (e1 * c1o[...] + e2 * c2o[...] + e3 * c3o[...]) / (e1 + e2 + e3)
    ycat = jnp.concatenate([gn(ya_ref[...], 0), gn(yb_ref[...], 1), gn(yc, 2), gn(yd_ref[...], 3)], axis=-1)
    xn = x_ref[...] + jnp.dot(ycat, wo_ref[...], preferred_element_type=F32)
    xo_ref[...] = xn
    hq = _rms_rows(xn, gx_ref[...]).astype(BF16)
    qx_ref[...] = (jnp.dot(hq, wq_ref[...], preferred_element_type=F32) * (X_HEAD_DIM ** -0.5)).astype(BF16)


def _mixout(x, ya, yb, c1, c2, c3, yd, gg, wo, gx, wq):
    n = x.shape[0]
    tm = TOKEN_TILE
    row = lambda wd: pl.BlockSpec((tm, wd), lambda i: (i, 0))
    return pl.pallas_call(
        _mixout_kernel,
        grid=(n // tm,),
        in_specs=[row(D_MODEL)] + [row(256)] * 9 + [
            _const_spec((4, GROUP_WIDTH)),
            _resident_spec((D_MODEL, D_MODEL)),
            _const_spec((1, D_MODEL)),
            _resident_spec((D_MODEL, D_MODEL)),
        ],
        out_specs=[row(D_MODEL), row(D_MODEL)],
        out_shape=[jax.ShapeDtypeStruct((n, D_MODEL), F32), jax.ShapeDtypeStruct((n, D_MODEL), BF16)],
        compiler_params=_cparams("parallel"),
        name="mixout",
    )(x, ya, yb, c1[0], c1[1], c2[0], c2[1], c3[0], c3[1], yd, gg, wo, gx, wq)


def _memkv_kernel(m_ref, g_ref, w_ref, kv_ref):
    h = _rms_rows(m_ref[...], g_ref[...]).astype(BF16)
    kv_ref[...] = jnp.dot(h, w_ref[...], preferred_element_type=F32).astype(BF16)


def _memkv(mem, g, w):
    n = mem.shape[0]
    return pl.pallas_call(
        _memkv_kernel,
        grid=(n // MEM_LEN,),
        in_specs=[
            pl.BlockSpec((MEM_LEN, D_MODEL), lambda i: (i, 0)),
            _const_spec((1, D_MODEL)),
            _resident_spec((D_MODEL, 2 * D_MODEL)),
        ],
        out_specs=pl.BlockSpec((MEM_LEN, 2 * D_MODEL), lambda i: (i, 0)),
        out_shape=jax.ShapeDtypeStruct((n, 2 * D_MODEL), BF16),
        compiler_params=_cparams("parallel"),
        name="memkv",
    )(mem, g, w)


def _cross_kernel(qx_ref, kv_ref, x_ref, wo_ref, xo_ref):
    outs = []
    for h in range(X_HEADS):
        cols = slice(h * X_HEAD_DIM, (h + 1) * X_HEAD_DIM)
        vcols = slice(D_MODEL + h * X_HEAD_DIM, D_MODEL + (h + 1) * X_HEAD_DIM)
        s = lax.dot_general(qx_ref[:, cols], kv_ref[:, cols], NT_DIMS, preferred_element_type=F32)
        m = jnp.max(s, axis=-1, keepdims=True)
        p = jnp.exp(s - m)
        l = jnp.sum(p, axis=-1, keepdims=True)
        o = jnp.dot(p.astype(BF16), kv_ref[:, vcols], preferred_element_type=F32) / l
        outs.append(o.astype(BF16))
    o = jnp.concatenate(outs, axis=-1)
    xo_ref[...] = x_ref[...] + jnp.dot(o, wo_ref[...], preferred_element_type=F32)


def _cross(qx, kv, x, wo, T):
    n = x.shape[0]
    tm = TOKEN_TILE
    tpb = T // tm
    row = lambda wd: pl.BlockSpec((tm, wd), lambda i: (i, 0))
    return pl.pallas_call(
        _cross_kernel,
        grid=(n // tm,),
        in_specs=[
            row(D_MODEL),
            pl.BlockSpec((MEM_LEN, 2 * D_MODEL), lambda i: (i // tpb, 0)),
            row(D_MODEL),
            _resident_spec((D_MODEL, D_MODEL)),
        ],
        out_specs=row(D_MODEL),
        out_shape=jax.ShapeDtypeStruct((n, D_MODEL), F32),
        compiler_params=_cparams("parallel"),
        name="cross",
    )(qx, kv, x, wo)


HALO = 16


def _ffn_kernel(xp_ref, x_ref, xn_ref, g_ref, wu_ref, cw_ref, cb_ref, wd_ref, gf_ref, o_ref, *, tpb, final):
    i = pl.program_id(0)
    tm = x_ref.shape[0]
    x = x_ref[...]
    xe = jnp.concatenate([xp_ref[...], x, xn_ref[...]], axis=0)
    he = _rms_rows(xe, g_ref[...]).astype(BF16)
    hc = he[HALO:HALO + tm]
    rows = lax.broadcasted_iota(jnp.int32, (tm + 2 * HALO, 1), 0)
    seq_first = (i % tpb) == 0
    seq_last = (i % tpb) == tpb - 1
    keep = jnp.logical_and(jnp.logical_or(rows >= HALO, jnp.logical_not(seq_first)),
                           jnp.logical_or(rows < HALO + tm, jnp.logical_not(seq_last)))
    keep = keep.astype(F32)
    acc = x
    for c in range(D_FF // FF_CHUNK):
        gc = slice(c * FF_CHUNK, (c + 1) * FF_CHUNK)
        uc = slice(D_FF + c * FF_CHUNK, D_FF + (c + 1) * FF_CHUNK)
        ge = jnp.dot(he, wu_ref[:, gc], preferred_element_type=F32) * keep
        val = jnp.dot(hc, wu_ref[:, uc], preferred_element_type=F32)
        gate = (ge[HALO - 1:HALO - 1 + tm] * cw_ref[0:1, gc] + ge[HALO:HALO + tm] * cw_ref[1:2, gc]
                + ge[HALO + 1:HALO + 1 + tm] * cw_ref[2:3, gc] + cb_ref[:, gc])
        act = 0.5 * gate * (1.0 + lax.erf(gate * (2.0 ** -0.5))) * val
        acc = acc + jnp.dot(act.astype(BF16), wd_ref[gc, :], preferred_element_type=F32)
    if final:
        acc = _rms_rows(acc, gf_ref[...])
    o_ref[...] = acc


def _ffn(x, g, wu, cw, cb, wd, gf, T, final):
    n = x.shape[0]
    tm = TOKEN_TILE
    tpb = T // tm
    hb = tm // HALO
    nhb = n // HALO
    return pl.pallas_call(
        functools.partial(_ffn_kernel, tpb=tpb, final=final),
        grid=(n // tm,),
        in_specs=[
            pl.BlockSpec((HALO, D_MODEL), lambda i: (jnp.maximum(i * hb - 1, 0), 0)),
            pl.BlockSpec((tm, D_MODEL), lambda i: (i, 0)),
            pl.BlockSpec((HALO, D_MODEL), lambda i: (jnp.minimum((i + 1) * hb, nhb - 1), 0)),
            _const_spec((1, D_MODEL)),
            _resident_spec((D_MODEL, 2 * D_FF)),
            _const_spec((3, D_FF)),
            _const_spec((1, D_FF)),
            _resident_spec((D_FF, D_MODEL)),
            _const_spec((1, D_MODEL)),
        ],
        out_specs=pl.BlockSpec((tm, D_MODEL), lambda i: (i, 0)),
        out_shape=jax.ShapeDtypeStruct((n, D_MODEL), F32),
        compiler_params=_cparams("parallel"),
        name="ffn",
    )(x, x, x, g, wu, cw, cb, wd, gf)


def _rope_tables(T):
    t = jnp.arange(T)

    def parts(pos, half):
        freqs = ROPE_THETA ** (-jnp.arange(half, dtype=F32) / half)
        ang = pos.astype(F32)[:, None] * freqs[None, :]
        c, s, z = jnp.cos(ang), jnp.sin(ang), jnp.zeros_like(ang)
        return jnp.concatenate([c, c], -1), jnp.concatenate([-s, z], -1), jnp.concatenate([z, s], -1)

    lin = parts(t, HEAD_DIM // 2)
    row = parts(t // GRID_W, HEAD_DIM // 4)
    colp = parts(t % GRID_W, HEAD_DIM // 4)
    ax = [jnp.concatenate([a, b], -1) for a, b in zip(row, colp)]
    return jnp.concatenate([jnp.tile(z, (1, 2)) for z in (*lin, *ax)], axis=-1)


def _na_bias_table(rpb):
    c = np.arange(GRID_W)
    qstart = np.clip(c - NA_COLS // 2, 0, GRID_W - NA_COLS)
    rel = c[None, :] - qstart[:, None]
    valid = (rel >= 0) & (rel < NA_COLS)
    dc = np.clip(c[None, :] - c[:, None], -(NA_COLS - 1), NA_COLS - 1) + (NA_COLS - 1)
    e = np.arange(NA_ROWS)
    m = np.arange(NA_ROWS)
    dr = m[None, :] - e[:, None] + (NA_ROWS - 1)
    tbl = rpb[:, dr[:, :, None, None], dc[None, None, :, :]]
    tbl = jnp.where(valid[None, None, None], tbl, NEG)
    tbl = tbl.transpose(1, 0, 3, 2, 4)
    return tbl.reshape(NA_ROWS, 2, 2 * GRID_W, NA_ROWS * GRID_W).astype(F32)


def _gqa_lane_order():
    return np.concatenate([np.arange(HEAD_DIM) + HEAD_DIM * h for h in GQA_HEAD_ORDER])


def _in_column_order():
    cols = np.arange(IN_WIDTH)
    order = _gqa_lane_order()
    cols[0:256] = order
    cols[512:768] = 512 + order
    return cols


def _trunk(x, mem, p):
    B, T, _ = x.shape
    n = B * T
    x = x.reshape(n, D_MODEL)
    tab = _rope_tables(T)
    bd = jnp.asarray(np.kron(np.eye(2), np.ones((HEAD_DIM, HEAD_DIM))), BF16)
    mem2 = mem.reshape(B * MEM_LEN, D_MODEL)
    for l in range(DEPTH):
        qa, ka, va, qb, kb, vb, qc, kc, vc, qd, kd, vd = _proj(
            x, p["norm_mix_g"][l], p["w_in"][l], tab, p["gq"][l], p["gk"][l], bd, T)
        ya = _attn_a(qa, ka, va, B, T)
        yb = _band(qb, kb, vb, nseq=B, L=T, S=B_RADIUS, W=3 * B_RADIUS, R=B_RADIUS, lead=1,
                   mode="sink", extra=p["sink"][l])
        cs = []
        for window, d in C_BRANCHES:
            r = window // (2 * d)
            sub = (lambda z: z) if d == 1 else (lambda z: _to_sub(z, B, T, d))
            unsub = (lambda z: z) if d == 1 else (lambda z: _from_sub(z, B, T, d))
            o, lse = _band(sub(qc), sub(kc), sub(vc), nseq=B * d, L=T // d, S=r, W=3 * r, R=r, lead=1,
                           mode="lse")
            cs.append((unsub(o), unsub(lse)))
        yd = _band(qd, kd, vd, nseq=B, L=T, S=GRID_W, W=NA_ROWS * GRID_W, R=0, lead=NA_ROWS // 2,
                   mode="bias", extra=p["na_bias"][l])
        x, qx = _mixout(x, ya, yb, cs[0], cs[1], cs[2], yd, p["grp_g"][l], p["w_out"][l],
                        p["norm_x_g"][l], p["w_xq"][l])
        kv = _memkv(mem2, p["norm_mem_g"][l], p["w_xkv"][l])
        x = _cross(qx, kv, x, p["w_xo"][l], T)
        x = _ffn(x, p["norm_ffn_g"][l], p["w_up"][l], p["conv_w"][l], p["conv_b"][l], p["w_down"][l],
                 p["final_g"], T, final=(l == DEPTH - 1))
    return x.reshape(B, T, D_MODEL)


def _prepare(norm_mix_g, w_in, a_q_norm_g, a_k_norm_g, b_sink, d_rpb, grp_norm_g, w_out, norm_x_g,
             norm_mem_g, w_xq, w_xkv, w_xo, norm_ffn_g, w_up, conv_w, conv_b, w_down, final_norm_g):
    order = _gqa_lane_order()
    row_order = np.arange(D_MODEL)
    row_order[0:256] = order
    row_order[256:512] = 256 + order
    grp_g = grp_norm_g.at[:, 0].set(grp_norm_g[:, 0][:, order]).at[:, 1].set(grp_norm_g[:, 1][:, order])
    sink = b_sink[:, np.array(GQA_HEAD_ORDER)].reshape(DEPTH, 2, 2)
    return {
        "norm_mix_g": norm_mix_g[:, None, :],
        "w_in": w_in[:, :, _in_column_order()].astype(BF16),
        "gq": jnp.tile(a_q_norm_g, (1, 2))[:, None, :] * (HEAD_DIM ** -0.5),
        "gk": jnp.tile(a_k_norm_g, (1, 2))[:, None, :],
        "sink": sink,
        "na_bias": jax.vmap(_na_bias_table)(d_rpb),
        "grp_g": grp_g,
        "w_out": w_out[:, row_order, :].astype(BF16),
        "norm_x_g": norm_x_g[:, None, :],
        "norm_mem_g": norm_mem_g[:, None, :],
        "w_xq": w_xq.astype(BF16),
        "w_xkv": w_xkv.astype(BF16),
        "w_xo": w_xo.astype(BF16),
        "norm_ffn_g": norm_ffn_g[:, None, :],
        "w_up": w_up.astype(BF16),
        "conv_w": conv_w,
        "conv_b": conv_b[:, None, :],
        "w_down": w_down.astype(BF16),
        "final_g": final_norm_g[None, :],
    }


def kernel(x_prompt, x_sample, mem_prompt, mem_sample, norm_mix_g, w_in, a_q_norm_g, a_k_norm_g, b_sink,
           d_rpb, grp_norm_g, w_out, norm_x_g, norm_mem_g, w_xq, w_xkv, w_xo, norm_ffn_g, w_up, conv_w,
           conv_b, w_down, final_norm_g):
    p = _prepare(norm_mix_g, w_in, a_q_norm_g, a_k_norm_g, b_sink, d_rpb, grp_norm_g, w_out, norm_x_g,
                 norm_mem_g, w_xq, w_xkv, w_xo, norm_ffn_g, w_up, conv_w, conv_b, w_down, final_norm_g)
    return _trunk(x_prompt, mem_prompt, p), _trunk(x_sample, mem_sample, p)
```

```python
import functools
import math

import numpy as np
import jax
import jax.numpy as jnp
from jax import lax
from jax.experimental import pallas as pl
from jax.experimental.pallas import tpu as pltpu

F32 = jnp.float32
BF16 = jnp.bfloat16

D_MODEL = 1024
DEPTH = 4
HEAD_DIM = 64
GROUP_WIDTH = 256
GRID_W = 64
ROPE_THETA = 10000.0
B_RADIUS = 128
C_BRANCHES = ((128, 1), (512, 4), (2048, 16))
NA_ROWS = 8
NA_COLS = 16
MEM_LEN = 256
X_HEADS = 4
X_HEAD_DIM = D_MODEL // X_HEADS
D_FF = 2816
EPS = 1e-6
NEG = -1e30
IN_WIDTH = 2560

LANES = 128
VMEM_LIMIT = 56 * 1024 * 1024
TOKEN_TILE = 1024
FFN_TILE = 512
BAND_TOKENS = 2048
A_TQ = 512
A_TK = 2048
A_ROWS = 128
BAND_BLOCK = 256
BAND_UNROLL_ROWS = 1024
LOG2E = 1.4426950408889634
FF_CHUNK = 1408

GQA_HEAD_ORDER = (0, 2, 1, 3)
NT_DIMS = (((1,), (1,)), ((), ()))


def _cparams(*sem):
    return pltpu.CompilerParams(dimension_semantics=sem, vmem_limit_bytes=VMEM_LIMIT)


def _const_spec(shape):
    n = len(shape)
    return pl.BlockSpec(shape, lambda *_: (0,) * n)


def _resident_spec(shape):
    n = len(shape)
    return pl.BlockSpec(shape, lambda *_: (0,) * n, pipeline_mode=pl.Buffered(1))


def _rms_rows(x, g):
    return x * lax.rsqrt(jnp.mean(x * x, axis=-1, keepdims=True) + EPS) * g


def _low_lanes():
    return lax.broadcasted_iota(jnp.int32, (1, LANES), 1) < HEAD_DIM


def _stack_heads(qb, low):
    zero = jnp.zeros_like(qb)
    return jnp.concatenate([jnp.where(low, qb, zero), jnp.where(low, zero, qb)], axis=0)


def _rope(y, c, sa, sb, shift):
    up = pltpu.roll(y, LANES - shift, 1)
    dn = pltpu.roll(y, shift, 1)
    return y * c + up * sa + dn * sb


def _proj_kernel(x_ref, g_ref, w_ref, tab_ref, gq_ref, gk_ref, bd_ref,
                 qa, ka, va, qb, kb, vb, qc, kc, vc, qd, kd, vd):
    h = _rms_rows(x_ref[...], g_ref[...]).astype(BF16)

    def mm(c0):
        return jnp.dot(h, w_ref[:, c0:c0 + 256], preferred_element_type=F32)

    def tab(k):
        return tab_ref[:, k * LANES:(k + 1) * LANES]

    def rope_lin(y):
        return _rope(y, tab(0), tab(1), tab(2), HEAD_DIM // 2)

    def rope_ax(y):
        return _rope(y, tab(3), tab(4), tab(5), HEAD_DIM // 4)

    def qknorm(y, g):
        ss = jnp.dot((y * y).astype(BF16), bd_ref[...], preferred_element_type=F32) * (1.0 / HEAD_DIM)
        return y * lax.rsqrt(ss + EPS) * g

    def halves(y):
        return y[:, :LANES], y[:, LANES:]

    scale = HEAD_DIM ** -0.5 * LOG2E

    y0, y1 = halves(mm(0))
    qa[:, :LANES] = rope_ax(qknorm(y0, gq_ref[...])).astype(BF16)
    qa[:, LANES:] = rope_ax(qknorm(y1, gq_ref[...])).astype(BF16)
    y0, y1 = halves(mm(256))
    ka[...] = rope_ax(qknorm(y0, gk_ref[...])).astype(BF16)
    va[...] = y1.astype(BF16)

    y0, y1 = halves(mm(512))
    qb[:, :LANES] = rope_lin(y0 * scale).astype(BF16)
    qb[:, LANES:] = rope_lin(y1 * scale).astype(BF16)
    y0, y1 = halves(mm(768))
    kb[...] = rope_lin(y0).astype(BF16)
    vb[...] = y1.astype(BF16)

    y0, y1 = halves(mm(1024))
    qc[:, :LANES] = rope_lin(y0 * scale).astype(BF16)
    qc[:, LANES:] = rope_lin(y1 * scale).astype(BF16)
    y0, y1 = halves(mm(1280))
    kc[:, :LANES] = rope_lin(y0).astype(BF16)
    kc[:, LANES:] = rope_lin(y1).astype(BF16)
    vc[...] = mm(1536).astype(BF16)

    qd[...] = (mm(1792) * scale).astype(BF16)
    kd[...] = mm(2048).astype(BF16)
    vd[...] = mm(2304).astype(BF16)


def _proj(x, g, w, tab, gq, gk, bd, T):
    n = x.shape[0]
    tm = TOKEN_TILE
    tpb = T // tm
    widths = (256, 128, 128, 256, 128, 128, 256, 256, 256, 256, 256, 256)
    return pl.pallas_call(
        _proj_kernel,
        grid=(n // tm,),
        in_specs=[
            pl.BlockSpec((tm, D_MODEL), lambda i: (i, 0)),
            _const_spec((1, D_MODEL)),
            _resident_spec((D_MODEL, IN_WIDTH)),
            pl.BlockSpec((tm, 6 * LANES), lambda i: (i % tpb, 0)),
            _const_spec((1, LANES)),
            _const_spec((1, LANES)),
            _const_spec((LANES, LANES)),
        ],
        out_specs=[pl.BlockSpec((tm, wd), lambda i: (i, 0)) for wd in widths],
        out_shape=[jax.ShapeDtypeStruct((n, wd), BF16) for wd in widths],
        compiler_params=_cparams("parallel"),
        name="proj",
    )(x, g, w, tab, gq, gk, bd)


def _attn_a_kernel(q_ref, k_ref, v_ref, o_ref, m_sc, acc_sc, *, nk):
    j = pl.program_id(2)
    tq = q_ref.shape[0]
    tk = k_ref.shape[0]

    @pl.when(j == 0)
    def _():
        m_sc[...] = jnp.full(m_sc.shape, NEG, F32)
        acc_sc[...] = jnp.zeros(acc_sc.shape, F32)

    low = _low_lanes()
    k = k_ref[...]
    v1 = jnp.concatenate([v_ref[...], jnp.ones((tk, LANES), BF16)], axis=1)
    for jb in range(2):
        q2 = _stack_heads(q_ref[:, jb * LANES:(jb + 1) * LANES], low)
        for rb in range(2 * tq // A_ROWS):
            rows = slice(rb * A_ROWS, (rb + 1) * A_ROWS)
            s = lax.dot_general(q2[rows], k, NT_DIMS, preferred_element_type=F32)
            m = m_sc[jb, rows, :]
            m_new = jnp.maximum(m, jnp.max(s, axis=-1, keepdims=True))
            p = jnp.exp2(s - m_new).astype(BF16)
            acc_sc[jb, rows, :] = (acc_sc[jb, rows, :] * jnp.exp2(m - m_new)
                                   + jnp.dot(p, v1, preferred_element_type=F32))
            m_sc[jb, rows, :] = m_new

    @pl.when(j == nk - 1)
    def _():
        for jb in range(2):
            cols = slice(jb * LANES, (jb + 1) * LANES)
            acc = acc_sc[jb]
            o2 = acc[:, :LANES] / acc[:, LANES:LANES + 1]
            o_ref[:, cols] = jnp.where(low, o2[:tq], o2[tq:]).astype(o_ref.dtype)


def _attn_a(q, k, v, B, T):
    n = q.shape[0]
    tq = A_TQ
    tk = min(A_TK, T)
    nq, nk = T // tq, T // tk
    return pl.pallas_call(
        functools.partial(_attn_a_kernel, nk=nk),
        grid=(B, nq, nk),
        in_specs=[
            pl.BlockSpec((tq, 256), lambda b, i, j: (b * nq + i, 0)),
            pl.BlockSpec((tk, LANES), lambda b, i, j: (b * nk + j, 0)),
            pl.BlockSpec((tk, LANES), lambda b, i, j: (b * nk + j, 0)),
        ],
        out_specs=pl.BlockSpec((tq, 256), lambda b, i, j: (b * nq + i, 0)),
        out_shape=jax.ShapeDtypeStruct((n, 256), BF16),
        scratch_shapes=[
            pltpu.VMEM((2, 2 * tq, 1), F32),
            pltpu.VMEM((2, 2 * tq, 256), F32),
        ],
        compiler_params=_cparams("parallel", "parallel", "arbitrary"),
        name="attn_a",
    )(q, k, v)


def _band_kernel(*refs, S, W, halo, align, L, kvw, mode, unroll):
    if mode == "sink":
        sink_ref, q_ref, k_ref, v_ref, tab_ref, o_ref = refs
    elif mode == "lse":
        q_ref, k_ref, v_ref, tab_ref, o_ref, lse_ref = refs
    else:
        q_ref, k_ref, v_ref, tab_ref, o_ref = refs
    qi = pl.program_id(1)
    G, TQ = q_ref.shape[0], q_ref.shape[1]
    per = TQ // S
    low = _low_lanes()
    ones = jnp.ones((W, LANES), BF16)
    first_half = lax.broadcasted_iota(jnp.int32, (2 * S, 1), 0) < S

    def body(it, carry):
        g = it // per
        i = it % per
        ig = qi * per + i
        start = pl.multiple_of(jnp.clip(ig * S - halo, 0, L - W), align)
        q0 = pl.multiple_of(i * S, S)
        qs = q_ref[g, pl.ds(q0, S), :]
        ks = k_ref[g, pl.ds(start, W), :]
        vs = v_ref[g, pl.ds(start, W), :]
        case = (ig * S - start) // (S if mode == "bias" else halo)
        for jb in range(2):
            cols = slice(jb * LANES, (jb + 1) * LANES)
            kcols = cols if kvw == 256 else slice(0, LANES)
            q2 = _stack_heads(qs[:, cols], low)
            s = lax.dot_general(q2, ks[:, kcols], NT_DIMS, preferred_element_type=F32)
            s = s + (tab_ref[case, jb] if mode == "bias" else tab_ref[case])
            m = jnp.max(s, axis=-1, keepdims=True)
            if mode == "sink":
                sk = jnp.where(first_half, sink_ref[jb, 0], sink_ref[jb, 1])
                m = jnp.maximum(m, sk)
            p = jnp.exp2(s - m).astype(BF16)
            v1 = jnp.concatenate([vs[:, kcols], ones], axis=1)
            pv = jnp.dot(p, v1, preferred_element_type=F32)
            l = pv[:, LANES:LANES + 1]
            if mode == "sink":
                l = l + jnp.exp2(sk - m)
            o2 = pv[:, :LANES] / l
            o_ref[g, pl.ds(q0, S), cols] = jnp.where(low, o2[:S], o2[S:]).astype(o_ref.dtype)
            if mode == "lse":
                lse = (m + jnp.log2(l)) * (1.0 / LOG2E)
                lse_ref[g, pl.ds(q0, S), cols] = jnp.where(low, lse[:S], lse[S:])
        return carry

    lax.fori_loop(0, G * per, body, 0, unroll=unroll)


def _band_mask_table(S, W, R, halo):
    rowq = np.arange(2 * S) % S
    col = np.arange(W)
    tabs = [np.where(np.abs(col[None, :] - rowq[:, None] - c * halo) <= R, 0.0, NEG) for c in range(3)]
    return jnp.asarray(np.stack(tabs), F32)


def _band(q, k, v, *, nseq, L, S, W, R, halo, mode, extra=None):
    kvw = k.shape[1]
    W = min(W, L)
    TQ = min(L, BAND_TOKENS)
    G = max(1, BAND_TOKENS // L)
    S = min(S, L)
    align = math.gcd(S, halo, L - W) if L > W else S
    unroll = max(1, BAND_UNROLL_ROWS // S)
    assert nseq % G == 0 and L % TQ == 0 and TQ % S == 0 and align % 16 == 0
    q3 = q.reshape(nseq, L, 256)
    k3 = k.reshape(nseq, L, kvw)
    v3 = v.reshape(nseq, L, kvw)
    qspec = pl.BlockSpec((G, TQ, 256), lambda s, i: (s, i, 0))
    kvspec = pl.BlockSpec((G, L, kvw), lambda s, i: (s, 0, 0))
    in_specs = [qspec, kvspec, kvspec]
    args = [q3, k3, v3]
    out_specs = qspec
    out_shape = jax.ShapeDtypeStruct((nseq, L, 256), BF16)
    tab = extra if mode == "bias" else _band_mask_table(S, W, R, halo)
    in_specs = in_specs + [_resident_spec(tab.shape)]
    args = args + [tab]
    if mode == "sink":
        in_specs = [pl.BlockSpec(memory_space=pltpu.SMEM)] + in_specs
        args = [extra] + args
    elif mode == "lse":
        out_specs = [qspec, qspec]
        out_shape = [out_shape, jax.ShapeDtypeStruct((nseq, L, 256), F32)]
    out = pl.pallas_call(
        functools.partial(_band_kernel, S=S, W=W, halo=halo, align=align, L=L, kvw=kvw, mode=mode,
                          unroll=unroll),
        grid=(nseq // G, L // TQ),
        in_specs=in_specs,
        out_specs=out_specs,
        out_shape=out_shape,
        compiler_params=_cparams("parallel", "arbitrary"),
        name="band_" + mode,
    )(*args)
    if mode == "lse":
        return out[0].reshape(nseq * L, 256), out[1].reshape(nseq * L, 256)
    return out.reshape(nseq * L, 256)


def _to_sub(z, B, T, d):
    w = z.shape[1]
    return z.reshape(B, T // d, d, w).transpose(0, 2, 1, 3).reshape(B * T, w)


def _from_sub(z, B, T, d):
    w = z.shape[1]
    return z.reshape(B, d, T // d, w).transpose(0, 2, 1, 3).reshape(B * T, w)


def _mixout_kernel(x_ref, ya_ref, yb_ref, c1o, c1l, c2o, c2l, c3o, c3l, yd_ref,
                   gg_ref, wo_ref, gx_ref, wq_ref, xo_ref, qx_ref):
    def gn(y, gi):
        return _rms_rows(y.astype(F32), gg_ref[gi:gi + 1, :]).astype(BF16)

    l1, l2, l3 = c1l[...], c2l[...], c3l[...]
    mx = jnp.maximum(jnp.maximum(l1, l2), l3)
    e1, e2, e3 = jnp.exp(l1 - mx), jnp.exp(l2 - mx), jnp.exp(l3 - mx)
    yc = (e1 * c1o[...].astype(F32) + e2 * c2o[...].astype(F32) + e3 * c3o[...].astype(F32)) / (e1 + e2 + e3)
    ycat = jnp.concatenate([gn(ya_ref[...], 0), gn(yb_ref[...], 1), gn(yc, 2), gn(yd_ref[...], 3)], axis=-1)
    xn = x_ref[...] + jnp.dot(ycat, wo_ref[...], preferred_element_type=F32)
    xo_ref[...] = xn
    hq = _rms_rows(xn, gx_ref[...]).astype(BF16)
    qx_ref[...] = (jnp.dot(hq, wq_ref[...], preferred_element_type=F32) * (X_HEAD_DIM ** -0.5)).astype(BF16)


def _mixout(x, ya, yb, c1, c2, c3, yd, gg, wo, gx, wq):
    n = x.shape[0]
    tm = TOKEN_TILE
    row = lambda wd: pl.BlockSpec((tm, wd), lambda i: (i, 0))
    return pl.pallas_call(
        _mixout_kernel,
        grid=(n // tm,),
        in_specs=[row(D_MODEL)] + [row(256)] * 9 + [
            _const_spec((4, GROUP_WIDTH)),
            _resident_spec((D_MODEL, D_MODEL)),
            _const_spec((1, D_MODEL)),
            _resident_spec((D_MODEL, D_MODEL)),
        ],
        out_specs=[row(D_MODEL), row(D_MODEL)],
        out_shape=[jax.ShapeDtypeStruct((n, D_MODEL), F32), jax.ShapeDtypeStruct((n, D_MODEL), BF16)],
        compiler_params=_cparams("parallel"),
        name="mixout",
    )(x, ya, yb, c1[0], c1[1], c2[0], c2[1], c3[0], c3[1], yd, gg, wo, gx, wq)


def _memkv_kernel(m_ref, g_ref, w_ref, kv_ref):
    h = _rms_rows(m_ref[...], g_ref[...]).astype(BF16)
    kv_ref[...] = jnp.dot(h, w_ref[...], preferred_element_type=F32).astype(BF16)


def _memkv(mem, g, w):
    n = mem.shape[0]
    return pl.pallas_call(
        _memkv_kernel,
        grid=(n // MEM_LEN,),
        in_specs=[
            pl.BlockSpec((MEM_LEN, D_MODEL), lambda i: (i, 0)),
            _const_spec((1, D_MODEL)),
            _resident_spec((D_MODEL, 2 * D_MODEL)),
        ],
        out_specs=pl.BlockSpec((MEM_LEN, 2 * D_MODEL), lambda i: (i, 0)),
        out_shape=jax.ShapeDtypeStruct((n, 2 * D_MODEL), BF16),
        compiler_params=_cparams("parallel"),
        name="memkv",
    )(mem, g, w)


def _cross_kernel(qx_ref, kv_ref, x_ref, wo_ref, xo_ref):
    outs = []
    for h in range(X_HEADS):
        cols = slice(h * X_HEAD_DIM, (h + 1) * X_HEAD_DIM)
        vcols = slice(D_MODEL + h * X_HEAD_DIM, D_MODEL + (h + 1) * X_HEAD_DIM)
        s = lax.dot_general(qx_ref[:, cols], kv_ref[:, cols], NT_DIMS, preferred_element_type=F32)
        m = jnp.max(s, axis=-1, keepdims=True)
        p = jnp.exp(s - m)
        l = jnp.sum(p, axis=-1, keepdims=True)
        o = jnp.dot(p.astype(BF16), kv_ref[:, vcols], preferred_element_type=F32) / l
        outs.append(o.astype(BF16))
    o = jnp.concatenate(outs, axis=-1)
    xo_ref[...] = x_ref[...] + jnp.dot(o, wo_ref[...], preferred_element_type=F32)


def _cross(qx, kv, x, wo, T):
    n = x.shape[0]
    tm = TOKEN_TILE
    tpb = T // tm
    row = lambda wd: pl.BlockSpec((tm, wd), lambda i: (i, 0))
    return pl.pallas_call(
        _cross_kernel,
        grid=(n // tm,),
        in_specs=[
            row(D_MODEL),
            pl.BlockSpec((MEM_LEN, 2 * D_MODEL), lambda i: (i // tpb, 0)),
            row(D_MODEL),
            _resident_spec((D_MODEL, D_MODEL)),
        ],
        out_specs=row(D_MODEL),
        out_shape=jax.ShapeDtypeStruct((n, D_MODEL), F32),
        compiler_params=_cparams("parallel"),
        name="cross",
    )(qx, kv, x, wo)


HALO = 16


def _ffn_kernel(xp_ref, x_ref, xn_ref, g_ref, wu_ref, cw_ref, cb_ref, wd_ref, gf_ref, o_ref, *, tpb, final):
    i = pl.program_id(0)
    tm = x_ref.shape[0]
    x = x_ref[...]
    xe = jnp.concatenate([xp_ref[...], x, xn_ref[...]], axis=0)
    he = _rms_rows(xe, g_ref[...]).astype(BF16)
    hc = he[HALO:HALO + tm]
    rows = lax.broadcasted_iota(jnp.int32, (tm + 2 * HALO, 1), 0)
    seq_first = (i % tpb) == 0
    seq_last = (i % tpb) == tpb - 1
    keep = jnp.logical_and(jnp.logical_or(rows >= HALO, jnp.logical_not(seq_first)),
                           jnp.logical_or(rows < HALO + tm, jnp.logical_not(seq_last)))
    keep = keep.astype(F32)
    acc = x
    for c in range(D_FF // FF_CHUNK):
        gc = slice(c * FF_CHUNK, (c + 1) * FF_CHUNK)
        uc = slice(D_FF + c * FF_CHUNK, D_FF + (c + 1) * FF_CHUNK)
        ge = jnp.dot(he, wu_ref[:, gc], preferred_element_type=F32) * keep
        val = jnp.dot(hc, wu_ref[:, uc], preferred_element_type=F32)
        gate = (ge[HALO - 1:HALO - 1 + tm] * cw_ref[0:1, gc] + ge[HALO:HALO + tm] * cw_ref[1:2, gc]
                + ge[HALO + 1:HALO + 1 + tm] * cw_ref[2:3, gc] + cb_ref[:, gc])
        act = 0.5 * gate * (1.0 + lax.erf(gate * (2.0 ** -0.5))) * val
        acc = acc + jnp.dot(act.astype(BF16), wd_ref[gc, :], preferred_element_type=F32)
    if final:
        acc = _rms_rows(acc, gf_ref[...])
    o_ref[...] = acc


def _ffn(x, g, wu, cw, cb, wd, gf, T, final):
    n = x.shape[0]
    tm = FFN_TILE
    tpb = T // tm
    hb = tm // HALO
    nhb = n // HALO
    return pl.pallas_call(
        functools.partial(_ffn_kernel, tpb=tpb, final=final),
        grid=(n // tm,),
        in_specs=[
            pl.BlockSpec((HALO, D_MODEL), lambda i: (jnp.maximum(i * hb - 1, 0), 0)),
            pl.BlockSpec((tm, D_MODEL), lambda i: (i, 0)),
            pl.BlockSpec((HALO, D_MODEL), lambda i: (jnp.minimum((i + 1) * hb, nhb - 1), 0)),
            _const_spec((1, D_MODEL)),
            _resident_spec((D_MODEL, 2 * D_FF)),
            _const_spec((3, D_FF)),
            _const_spec((1, D_FF)),
            _resident_spec((D_FF, D_MODEL)),
            _const_spec((1, D_MODEL)),
        ],
        out_specs=pl.BlockSpec((tm, D_MODEL), lambda i: (i, 0)),
        out_shape=jax.ShapeDtypeStruct((n, D_MODEL), F32),
        compiler_params=_cparams("parallel"),
        name="ffn",
    )(x, x, x, g, wu, cw, cb, wd, gf)


def _rope_tables(T):
    t = jnp.arange(T)

    def parts(pos, half):
        freqs = ROPE_THETA ** (-jnp.arange(half, dtype=F32) / half)
        ang = pos.astype(F32)[:, None] * freqs[None, :]
        c, s, z = jnp.cos(ang), jnp.sin(ang), jnp.zeros_like(ang)
        return jnp.concatenate([c, c], -1), jnp.concatenate([-s, z], -1), jnp.concatenate([z, s], -1)

    lin = parts(t, HEAD_DIM // 2)
    row = parts(t // GRID_W, HEAD_DIM // 4)
    colp = parts(t % GRID_W, HEAD_DIM // 4)
    ax = [jnp.concatenate([a, b], -1) for a, b in zip(row, colp)]
    return jnp.concatenate([jnp.tile(z, (1, 2)) for z in (*lin, *ax)], axis=-1)


def _na_bias_table(rpb):
    c = np.arange(GRID_W)
    qstart = np.clip(c - NA_COLS // 2, 0, GRID_W - NA_COLS)
    rel = c[None, :] - qstart[:, None]
    valid = (rel >= 0) & (rel < NA_COLS)
    dc = np.clip(c[None, :] - c[:, None], -(NA_COLS - 1), NA_COLS - 1) + (NA_COLS - 1)
    e = np.arange(NA_ROWS)
    m = np.arange(NA_ROWS)
    dr = m[None, :] - e[:, None] + (NA_ROWS - 1)
    onehot = np.zeros((2 * NA_COLS - 1, GRID_W * GRID_W), np.float32)
    onehot[dc.reshape(-1), np.arange(GRID_W * GRID_W)] = 1.0
    rows = rpb[:, dr].reshape(-1, 2 * NA_COLS - 1)
    tbl = jnp.dot(rows, onehot, precision=lax.Precision.HIGHEST)
    tbl = tbl.reshape(rpb.shape[0], NA_ROWS, NA_ROWS, GRID_W, GRID_W)
    tbl = jnp.where(valid[None, None, None], tbl, NEG)
    tbl = tbl.transpose(1, 0, 3, 2, 4)
    return tbl.reshape(NA_ROWS, 2, 2 * GRID_W, NA_ROWS * GRID_W).astype(F32)


def _gqa_lane_order():
    return np.concatenate([np.arange(HEAD_DIM) + HEAD_DIM * h for h in GQA_HEAD_ORDER])


def _in_column_order():
    cols = np.arange(IN_WIDTH)
    order = _gqa_lane_order()
    cols[0:256] = order
    cols[512:768] = 512 + order
    return cols


def _trunk(x, mem, p):
    B, T, _ = x.shape
    n = B * T
    x = x.reshape(n, D_MODEL)
    tab = _rope_tables(T)
    bd = jnp.asarray(np.kron(np.eye(2), np.ones((HEAD_DIM, HEAD_DIM))), BF16)
    mem2 = mem.reshape(B * MEM_LEN, D_MODEL)
    for l in range(DEPTH):
        qa, ka, va, qb, kb, vb, qc, kc, vc, qd, kd, vd = _proj(
            x, p["norm_mix_g"][l], p["w_in"][l], tab, p["gq"][l], p["gk"][l], bd, T)
        ya = _attn_a(qa, ka, va, B, T)
        yb = _band(qb, kb, vb, nseq=B, L=T, S=BAND_BLOCK, W=BAND_BLOCK + 2 * B_RADIUS, R=B_RADIUS,
                   halo=B_RADIUS, mode="sink", extra=p["sink"][l])
        cs = []
        for window, d in C_BRANCHES:
            r = window // (2 * d)
            sub = (lambda z: z) if d == 1 else (lambda z: _to_sub(z, B, T, d))
            unsub = (lambda z: z) if d == 1 else (lambda z: _from_sub(z, B, T, d))
            blk = min(BAND_BLOCK, T // d)
            o, lse = _band(sub(qc), sub(kc), sub(vc), nseq=B * d, L=T // d, S=blk, W=blk + 2 * r, R=r,
                           halo=r, mode="lse")
            cs.append((unsub(o), unsub(lse)))
        yd = _band(qd, kd, vd, nseq=B, L=T, S=GRID_W, W=NA_ROWS * GRID_W, R=0, halo=(NA_ROWS // 2) * GRID_W,
                   mode="bias", extra=p["na_bias"][l])
        x, qx = _mixout(x, ya, yb, cs[0], cs[1], cs[2], yd, p["grp_g"][l], p["w_out"][l],
                        p["norm_x_g"][l], p["w_xq"][l])
        kv = _memkv(mem2, p["norm_mem_g"][l], p["w_xkv"][l])
        x = _cross(qx, kv, x, p["w_xo"][l], T)
        x = _ffn(x, p["norm_ffn_g"][l], p["w_up"][l], p["conv_w"][l], p["conv_b"][l], p["w_down"][l],
                 p["final_g"], T, final=(l == DEPTH - 1))
    return x.reshape(B, T, D_MODEL)


def _prepare(norm_mix_g, w_in, a_q_norm_g, a_k_norm_g, b_sink, d_rpb, grp_norm_g, w_out, norm_x_g,
             norm_mem_g, w_xq, w_xkv, w_xo, norm_ffn_g, w_up, conv_w, conv_b, w_down, final_norm_g):
    order = _gqa_lane_order()
    row_order = np.arange(D_MODEL)
    row_order[0:256] = order
    row_order[256:512] = 256 + order
    grp_g = grp_norm_g.at[:, 0].set(grp_norm_g[:, 0][:, order]).at[:, 1].set(grp_norm_g[:, 1][:, order])
    sink = b_sink[:, np.array(GQA_HEAD_ORDER)].reshape(-1, 2, 2) * LOG2E
    return {
        "norm_mix_g": norm_mix_g[:, None, :],
        "w_in": w_in[:, :, _in_column_order()].astype(BF16),
        "gq": jnp.tile(a_q_norm_g, (1, 2))[:, None, :] * (HEAD_DIM ** -0.5 * LOG2E),
        "gk": jnp.tile(a_k_norm_g, (1, 2))[:, None, :],
        "sink": sink,
        "na_bias": jax.vmap(_na_bias_table)(d_rpb * LOG2E),
        "grp_g": grp_g,
        "w_out": w_out[:, row_order, :].astype(BF16),
        "norm_x_g": norm_x_g[:, None, :],
        "norm_mem_g": norm_mem_g[:, None, :],
        "w_xq": w_xq.astype(BF16),
        "w_xkv": w_xkv.astype(BF16),
        "w_xo": w_xo.astype(BF16),
        "norm_ffn_g": norm_ffn_g[:, None, :],
        "w_up": w_up.astype(BF16),
        "conv_w": conv_w,
        "conv_b": conv_b[:, None, :],
        "w_down": w_down.astype(BF16),
        "final_g": final_norm_g[None, :],
    }


def kernel(x_prompt, x_sample, mem_prompt, mem_sample, norm_mix_g, w_in, a_q_norm_g, a_k_norm_g, b_sink,
           d_rpb, grp_norm_g, w_out, norm_x_g, norm_mem_g, w_xq, w_xkv, w_xo, norm_ffn_g, w_up, conv_w,
           conv_b, w_down, final_norm_g):
    p = _prepare(norm_mix_g, w_in, a_q_norm_g, a_k_norm_g, b_sink, d_rpb, grp_norm_g, w_out, norm_x_g,
                 norm_mem_g, w_xq, w_xkv, w_xo, norm_ffn_g, w_up, conv_w, conv_b, w_down, final_norm_g)
    return _trunk(x_prompt, mem_prompt, p), _trunk(x_sample, mem_sample, p)
```

```python
import functools
import math

import numpy as np
import jax
import jax.numpy as jnp
from jax import lax
from jax.experimental import pallas as pl
from jax.experimental.pallas import tpu as pltpu

F32 = jnp.float32
BF16 = jnp.bfloat16

D_MODEL = 1024
DEPTH = 4
HEAD_DIM = 64
GROUP_WIDTH = 256
GRID_W = 64
ROPE_THETA = 10000.0
B_RADIUS = 128
C_BRANCHES = ((128, 1), (512, 4), (2048, 16))
NA_ROWS = 8
NA_COLS = 16
MEM_LEN = 256
X_HEADS = 4
X_HEAD_DIM = D_MODEL // X_HEADS
D_FF = 2816
EPS = 1e-6
NEG = -1e30
IN_WIDTH = 2560

LANES = 128
VMEM_LIMIT = 56 * 1024 * 1024
TOKEN_TILE = 1024
FFN_TILE = 512
BAND_TOKENS = 2048
A_TQ = 512
A_TK = 4096
A_ROWS = 128
BAND_BLOCK = 256
C_BLOCK = 128
BAND_UNROLL_ROWS = 1024
LOG2E = 1.4426950408889634
FF_CHUNK = 1408

GQA_HEAD_ORDER = (0, 2, 1, 3)
NT_DIMS = (((1,), (1,)), ((), ()))


def _cparams(*sem):
    return pltpu.CompilerParams(dimension_semantics=sem, vmem_limit_bytes=VMEM_LIMIT)


def _const_spec(shape):
    n = len(shape)
    return pl.BlockSpec(shape, lambda *_: (0,) * n)


def _resident_spec(shape):
    n = len(shape)
    return pl.BlockSpec(shape, lambda *_: (0,) * n, pipeline_mode=pl.Buffered(1))


def _rms_rows(x, g):
    return x * lax.rsqrt(jnp.mean(x * x, axis=-1, keepdims=True) + EPS) * g


def _low_lanes():
    return lax.broadcasted_iota(jnp.int32, (1, LANES), 1) < HEAD_DIM


def _stack_heads(qb, low):
    zero = jnp.zeros_like(qb)
    return jnp.concatenate([jnp.where(low, qb, zero), jnp.where(low, zero, qb)], axis=0)


def _rope(y, c, sa, sb, shift):
    up = pltpu.roll(y, LANES - shift, 1)
    dn = pltpu.roll(y, shift, 1)
    return y * c + up * sa + dn * sb


def _proj_kernel(x_ref, g_ref, w_ref, tab_ref, gq_ref, gk_ref, bd_ref,
                 qa, ka, va, qb, kb, vb, qc, kc, vc, qc4, kc4, vc4, qc16, kc16, vc16, qd, kd, vd,
                 q_scr, k_scr, v_scr):
    tm = x_ref.shape[0]
    h = _rms_rows(x_ref[...], g_ref[...]).astype(BF16)

    def emit_c(scr, nat, subs):
        for hb in range(2):
            cols = slice(hb * LANES, (hb + 1) * LANES)
            nat[:, cols] = scr[hb].astype(BF16)
            for (_, d), ref in zip(C_BRANCHES[1:], subs):
                for r in range(d):
                    ref[0, r, :, cols] = scr[hb, pl.ds(r, tm // d, stride=d), :].astype(BF16)

    def mm(c0):
        return jnp.dot(h, w_ref[:, c0:c0 + 256], preferred_element_type=F32)

    def tab(k):
        return tab_ref[:, k * LANES:(k + 1) * LANES]

    def rope_lin(y):
        return _rope(y, tab(0), tab(1), tab(2), HEAD_DIM // 2)

    def rope_ax(y):
        return _rope(y, tab(3), tab(4), tab(5), HEAD_DIM // 4)

    def qknorm(y, g):
        ss = jnp.dot((y * y).astype(BF16), bd_ref[...], preferred_element_type=F32) * (1.0 / HEAD_DIM)
        return y * lax.rsqrt(ss + EPS) * g

    def halves(y):
        return y[:, :LANES], y[:, LANES:]

    scale = HEAD_DIM ** -0.5 * LOG2E

    y0, y1 = halves(mm(0))
    qa[:, :LANES] = rope_ax(qknorm(y0, gq_ref[...])).astype(BF16)
    qa[:, LANES:] = rope_ax(qknorm(y1, gq_ref[...])).astype(BF16)
    y0, y1 = halves(mm(256))
    ka[...] = rope_ax(qknorm(y0, gk_ref[...])).astype(BF16)
    va[...] = y1.astype(BF16)

    y0, y1 = halves(mm(512))
    qb[:, :LANES] = rope_lin(y0 * scale).astype(BF16)
    qb[:, LANES:] = rope_lin(y1 * scale).astype(BF16)
    y0, y1 = halves(mm(768))
    kb[...] = rope_lin(y0).astype(BF16)
    vb[...] = y1.astype(BF16)

    y0, y1 = halves(mm(1024))
    q_scr[0] = rope_lin(y0 * scale)
    q_scr[1] = rope_lin(y1 * scale)
    emit_c(q_scr, qc, (qc4, qc16))
    y0, y1 = halves(mm(1280))
    k_scr[0] = rope_lin(y0)
    k_scr[1] = rope_lin(y1)
    emit_c(k_scr, kc, (kc4, kc16))
    y0, y1 = halves(mm(1536))
    v_scr[0] = y0
    v_scr[1] = y1
    emit_c(v_scr, vc, (vc4, vc16))

    qd[...] = (mm(1792) * scale).astype(BF16)
    kd[...] = mm(2048).astype(BF16)
    vd[...] = mm(2304).astype(BF16)


def _proj(x, g, w, tab, gq, gk, bd, T):
    n = x.shape[0]
    tm = TOKEN_TILE
    tpb = T // tm
    B = n // T
    row = lambda wd: (pl.BlockSpec((tm, wd), lambda i: (i, 0)), jax.ShapeDtypeStruct((n, wd), BF16))

    def slab(d):
        return (pl.BlockSpec((1, d, tm // d, 256), lambda i: (i // tpb, 0, i % tpb, 0)),
                jax.ShapeDtypeStruct((B, d, T // d, 256), BF16))

    outs = ([row(256), row(128), row(128)] * 2 + [row(256)] * 3
            + [slab(d) for _, d in C_BRANCHES[1:] for _ in range(3)] + [row(256)] * 3)
    return pl.pallas_call(
        _proj_kernel,
        grid=(n // tm,),
        in_specs=[
            pl.BlockSpec((tm, D_MODEL), lambda i: (i, 0)),
            _const_spec((1, D_MODEL)),
            _resident_spec((D_MODEL, IN_WIDTH)),
            pl.BlockSpec((tm, 6 * LANES), lambda i: (i % tpb, 0)),
            _const_spec((1, LANES)),
            _const_spec((1, LANES)),
            _const_spec((LANES, LANES)),
        ],
        out_specs=[o[0] for o in outs],
        out_shape=[o[1] for o in outs],
        scratch_shapes=[pltpu.VMEM((2, tm, LANES), F32)] * 3,
        compiler_params=_cparams("parallel"),
        name="proj",
    )(x, g, w, tab, gq, gk, bd)


def _attn_a_kernel(q_ref, k_ref, v_ref, o_ref, m_sc, acc_sc, *, nk):
    j = pl.program_id(2)
    tq = q_ref.shape[0]
    tk = k_ref.shape[0]

    @pl.when(j == 0)
    def _():
        m_sc[...] = jnp.full(m_sc.shape, NEG, F32)
        acc_sc[...] = jnp.zeros(acc_sc.shape, F32)

    low = _low_lanes()
    k = k_ref[...]
    v1 = jnp.concatenate([v_ref[...], jnp.ones((tk, LANES), BF16)], axis=1)
    for jb in range(2):
        q2 = _stack_heads(q_ref[:, jb * LANES:(jb + 1) * LANES], low)
        for rb in range(2 * tq // A_ROWS):
            rows = slice(rb * A_ROWS, (rb + 1) * A_ROWS)
            s = lax.dot_general(q2[rows], k, NT_DIMS, preferred_element_type=F32)
            m = m_sc[jb, rows, :]
            m_new = jnp.maximum(m, jnp.max(s, axis=-1, keepdims=True))
            p = jnp.exp2(s - m_new).astype(BF16)
            acc_sc[jb, rows, :] = (acc_sc[jb, rows, :] * jnp.exp2(m - m_new)
                                   + jnp.dot(p, v1, preferred_element_type=F32))
            m_sc[jb, rows, :] = m_new

    @pl.when(j == nk - 1)
    def _():
        for jb in range(2):
            cols = slice(jb * LANES, (jb + 1) * LANES)
            acc = acc_sc[jb]
            o2 = acc[:, :LANES] / acc[:, LANES:]
            o_ref[:, cols] = jnp.where(low, o2[:tq], o2[tq:]).astype(o_ref.dtype)


def _attn_a(q, k, v, B, T):
    n = q.shape[0]
    tq = A_TQ
    tk = min(A_TK, T)
    nq, nk = T // tq, T // tk
    return pl.pallas_call(
        functools.partial(_attn_a_kernel, nk=nk),
        grid=(B, nq, nk),
        in_specs=[
            pl.BlockSpec((tq, 256), lambda b, i, j: (b * nq + i, 0)),
            pl.BlockSpec((tk, LANES), lambda b, i, j: (b * nk + j, 0)),
            pl.BlockSpec((tk, LANES), lambda b, i, j: (b * nk + j, 0)),
        ],
        out_specs=pl.BlockSpec((tq, 256), lambda b, i, j: (b * nq + i, 0)),
        out_shape=jax.ShapeDtypeStruct((n, 256), BF16),
        scratch_shapes=[
            pltpu.VMEM((2, 2 * tq, 1), F32),
            pltpu.VMEM((2, 2 * tq, 256), F32),
        ],
        compiler_params=_cparams("parallel", "parallel", "arbitrary"),
        name="attn_a",
    )(q, k, v)


def _band_kernel(*refs, S, W, halo, align, L, kvw, mode, unroll):
    if mode == "sink":
        sink_ref, q_ref, k_ref, v_ref, tab_ref, o_ref = refs
    elif mode == "lse":
        q_ref, k_ref, v_ref, tab_ref, o_ref, lse_ref = refs
    else:
        q_ref, k_ref, v_ref, tab_ref, o_ref = refs
    qi = pl.program_id(1)
    G, TQ = q_ref.shape[0], q_ref.shape[1]
    per = TQ // S
    low = _low_lanes()
    ones = jnp.ones((W, LANES), BF16)
    first_half = lax.broadcasted_iota(jnp.int32, (2 * S, 1), 0) < S

    def body(it, carry):
        g = it // per
        i = it % per
        ig = qi * per + i
        start = pl.multiple_of(jnp.clip(ig * S - halo, 0, L - W), align)
        q0 = pl.multiple_of(i * S, S)
        qs = q_ref[g, pl.ds(q0, S), :]
        ks = k_ref[g, pl.ds(start, W), :]
        vs = v_ref[g, pl.ds(start, W), :]
        case = (ig * S - start) // (S if mode == "bias" else halo)
        for jb in range(2):
            cols = slice(jb * LANES, (jb + 1) * LANES)
            kcols = cols if kvw == 256 else slice(0, LANES)
            q2 = _stack_heads(qs[:, cols], low)
            s = lax.dot_general(q2, ks[:, kcols], NT_DIMS, preferred_element_type=F32)
            s = s + (tab_ref[case, jb] if mode == "bias" else tab_ref[case])
            m = jnp.max(s, axis=-1, keepdims=True)
            if mode == "sink":
                sk = jnp.where(first_half, sink_ref[jb, 0], sink_ref[jb, 1])
                m = jnp.maximum(m, sk)
            p = jnp.exp2(s - m).astype(BF16)
            v1 = jnp.concatenate([vs[:, kcols], ones], axis=1)
            pv = jnp.dot(p, v1, preferred_element_type=F32)
            l = pv[:, LANES:]
            if mode == "sink":
                l = l + jnp.exp2(sk - m)
            o2 = pv[:, :LANES] / l
            o_ref[g, pl.ds(q0, S), cols] = jnp.where(low, o2[:S], o2[S:]).astype(o_ref.dtype)
            if mode == "lse":
                lse = (m + jnp.log2(l)) * (1.0 / LOG2E)
                lse_ref[g, pl.ds(q0, S), cols] = jnp.where(low, lse[:S], lse[S:])
        return carry

    lax.fori_loop(0, G * per, body, 0, unroll=unroll)


def _band_mask_table(S, W, R, halo):
    rowq = np.arange(2 * S) % S
    col = np.arange(W)
    tabs = [np.where(np.abs(col[None, :] - rowq[:, None] - c * halo) <= R, 0.0, NEG) for c in range(3)]
    return jnp.asarray(np.stack(tabs), F32)


def _band(q, k, v, *, nseq, L, S, W, R, halo, mode, extra=None):
    kvw = k.shape[1]
    W = min(W, L)
    TQ = min(L, BAND_TOKENS)
    G = max(1, BAND_TOKENS // L)
    S = min(S, L)
    align = math.gcd(S, halo, L - W) if L > W else S
    unroll = max(1, BAND_UNROLL_ROWS // S)
    assert nseq % G == 0 and L % TQ == 0 and TQ % S == 0 and align % 16 == 0
    q3 = q.reshape(nseq, L, 256)
    k3 = k.reshape(nseq, L, kvw)
    v3 = v.reshape(nseq, L, kvw)
    qspec = pl.BlockSpec((G, TQ, 256), lambda s, i: (s, i, 0))
    kvspec = pl.BlockSpec((G, L, kvw), lambda s, i: (s, 0, 0))
    in_specs = [qspec, kvspec, kvspec]
    args = [q3, k3, v3]
    out_specs = qspec
    out_shape = jax.ShapeDtypeStruct((nseq, L, 256), BF16)
    tab = extra if mode == "bias" else _band_mask_table(S, W, R, halo)
    in_specs = in_specs + [_resident_spec(tab.shape)]
    args = args + [tab]
    if mode == "sink":
        in_specs = [pl.BlockSpec(memory_space=pltpu.SMEM)] + in_specs
        args = [extra] + args
    elif mode == "lse":
        out_specs = [qspec, qspec]
        out_shape = [out_shape, jax.ShapeDtypeStruct((nseq, L, 256), F32)]
    out = pl.pallas_call(
        functools.partial(_band_kernel, S=S, W=W, halo=halo, align=align, L=L, kvw=kvw, mode=mode,
                          unroll=unroll),
        grid=(nseq // G, L // TQ),
        in_specs=in_specs,
        out_specs=out_specs,
        out_shape=out_shape,
        compiler_params=_cparams("parallel", "arbitrary"),
        name="band_" + mode,
    )(*args)
    if mode == "lse":
        return out[0].reshape(nseq * L, 256), out[1].reshape(nseq * L, 256)
    return out.reshape(nseq * L, 256)


def _mixout_kernel(x_ref, ya_ref, yb_ref, c1o, c1l, c2o, c2l, c3o, c3l, yd_ref,
                   gg_ref, wo_ref, gx_ref, wq_ref, xo_ref, qx_ref, *scr):
    tm = x_ref.shape[0]

    def gn(y, gi):
        return _rms_rows(y.astype(F32), gg_ref[gi:gi + 1, :]).astype(BF16)

    def token_order(ref, buf):
        d = ref.shape[1]
        for hb in range(2):
            for r in range(d):
                buf[hb, pl.ds(r, tm // d, stride=d), :] = ref[0, r, :, hb * LANES:(hb + 1) * LANES].astype(F32)
        return jnp.concatenate([buf[0], buf[1]], axis=-1)

    o1, l1 = c1o[...].astype(F32), c1l[...]
    o2, l2 = token_order(c2o, scr[0]), token_order(c2l, scr[1])
    o3, l3 = token_order(c3o, scr[2]), token_order(c3l, scr[3])
    mx = jnp.maximum(jnp.maximum(l1, l2), l3)
    e1, e2, e3 = jnp.exp(l1 - mx), jnp.exp(l2 - mx), jnp.exp(l3 - mx)
    yc = (e1 * o1 + e2 * o2 + e3 * o3) / (e1 + e2 + e3)
    ycat = jnp.concatenate([gn(ya_ref[...], 0), gn(yb_ref[...], 1), gn(yc, 2), gn(yd_ref[...], 3)], axis=-1)
    xn = x_ref[...] + jnp.dot(ycat, wo_ref[...], preferred_element_type=F32)
    xo_ref[...] = xn
    hq = _rms_rows(xn, gx_ref[...]).astype(BF16)
    qx_ref[...] = (jnp.dot(hq, wq_ref[...], preferred_element_type=F32) * (X_HEAD_DIM ** -0.5)).astype(BF16)


def _mixout(x, ya, yb, c1, c2, c3, yd, gg, wo, gx, wq, T):
    n = x.shape[0]
    B = n // T
    tm = TOKEN_TILE
    tpb = T // tm
    row = lambda wd: pl.BlockSpec((tm, wd), lambda i: (i, 0))
    slab = lambda d: pl.BlockSpec((1, d, tm // d, 256), lambda i: (i // tpb, 0, i % tpb, 0))
    d2, d3 = C_BRANCHES[1][1], C_BRANCHES[2][1]
    return pl.pallas_call(
        _mixout_kernel,
        grid=(n // tm,),
        in_specs=[row(D_MODEL)] + [row(256)] * 4 + [slab(d2)] * 2 + [slab(d3)] * 2 + [row(256)] + [
            _const_spec((4, GROUP_WIDTH)),
            _resident_spec((D_MODEL, D_MODEL)),
            _const_spec((1, D_MODEL)),
            _resident_spec((D_MODEL, D_MODEL)),
        ],
        out_specs=[row(D_MODEL), row(D_MODEL)],
        out_shape=[jax.ShapeDtypeStruct((n, D_MODEL), F32), jax.ShapeDtypeStruct((n, D_MODEL), BF16)],
        scratch_shapes=[pltpu.VMEM((2, tm, LANES), F32)] * 4,
        compiler_params=_cparams("parallel"),
        name="mixout",
    )(x, ya, yb, c1[0], c1[1], *[z.reshape(B, d2, T // d2, 256) for z in c2],
      *[z.reshape(B, d3, T // d3, 256) for z in c3], yd, gg, wo, gx, wq)


def _memkv_kernel(m_ref, g_ref, w_ref, kv_ref):
    h = _rms_rows(m_ref[...], g_ref[...]).astype(BF16)
    kv_ref[...] = jnp.dot(h, w_ref[...], preferred_element_type=F32).astype(BF16)


def _memkv(mem, g, w):
    n = mem.shape[0]
    return pl.pallas_call(
        _memkv_kernel,
        grid=(n // MEM_LEN,),
        in_specs=[
            pl.BlockSpec((MEM_LEN, D_MODEL), lambda i: (i, 0)),
            _const_spec((1, D_MODEL)),
            _resident_spec((D_MODEL, 2 * D_MODEL)),
        ],
        out_specs=pl.BlockSpec((MEM_LEN, 2 * D_MODEL), lambda i: (i, 0)),
        out_shape=jax.ShapeDtypeStruct((n, 2 * D_MODEL), BF16),
        compiler_params=_cparams("parallel"),
        name="memkv",
    )(mem, g, w)


def _cross_kernel(qx_ref, kv_ref, x_ref, wo_ref, xo_ref):
    outs = []
    for h in range(X_HEADS):
        cols = slice(h * X_HEAD_DIM, (h + 1) * X_HEAD_DIM)
        vcols = slice(D_MODEL + h * X_HEAD_DIM, D_MODEL + (h + 1) * X_HEAD_DIM)
        s = lax.dot_general(qx_ref[:, cols], kv_ref[:, cols], NT_DIMS, preferred_element_type=F32)
        m = jnp.max(s, axis=-1, keepdims=True)
        p = jnp.exp(s - m)
        l = jnp.sum(p, axis=-1, keepdims=True)
        o = jnp.dot(p.astype(BF16), kv_ref[:, vcols], preferred_element_type=F32) / l
        outs.append(o.astype(BF16))
    o = jnp.concatenate(outs, axis=-1)
    xo_ref[...] = x_ref[...] + jnp.dot(o, wo_ref[...], preferred_element_type=F32)


def _cross(qx, kv, x, wo, T):
    n = x.shape[0]
    tm = TOKEN_TILE
    tpb = T // tm
    row = lambda wd: pl.BlockSpec((tm, wd), lambda i: (i, 0))
    return pl.pallas_call(
        _cross_kernel,
        grid=(n // tm,),
        in_specs=[
            row(D_MODEL),
            pl.BlockSpec((MEM_LEN, 2 * D_MODEL), lambda i: (i // tpb, 0)),
            row(D_MODEL),
            _resident_spec((D_MODEL, D_MODEL)),
        ],
        out_specs=row(D_MODEL),
        out_shape=jax.ShapeDtypeStruct((n, D_MODEL), F32),
        compiler_params=_cparams("parallel"),
        name="cross",
    )(qx, kv, x, wo)


HALO = 16


def _ffn_kernel(xp_ref, x_ref, xn_ref, g_ref, wu_ref, cw_ref, cb_ref, wd_ref, gf_ref, o_ref, *, tpb, final):
    i = pl.program_id(0)
    tm = x_ref.shape[0]
    x = x_ref[...]
    xe = jnp.concatenate([xp_ref[...], x, xn_ref[...]], axis=0)
    he = _rms_rows(xe, g_ref[...]).astype(BF16)
    hc = he[HALO:HALO + tm]
    rows = lax.broadcasted_iota(jnp.int32, (tm + 2 * HALO, 1), 0)
    seq_first = (i % tpb) == 0
    seq_last = (i % tpb) == tpb - 1
    keep = jnp.logical_and(jnp.logical_or(rows >= HALO, jnp.logical_not(seq_first)),
                           jnp.logical_or(rows < HALO + tm, jnp.logical_not(seq_last)))
    keep = keep.astype(F32)
    acc = x
    for c in range(D_FF // FF_CHUNK):
        gc = slice(c * FF_CHUNK, (c + 1) * FF_CHUNK)
        uc = slice(D_FF + c * FF_CHUNK, D_FF + (c + 1) * FF_CHUNK)
        ge = jnp.dot(he, wu_ref[:, gc], preferred_element_type=F32) * keep
        val = jnp.dot(hc, wu_ref[:, uc], preferred_element_type=F32)
        gate = (ge[HALO - 1:HALO - 1 + tm] * cw_ref[0:1, gc] + ge[HALO:HALO + tm] * cw_ref[1:2, gc]
                + ge[HALO + 1:HALO + 1 + tm] * cw_ref[2:3, gc] + cb_ref[:, gc])
        act = 0.5 * gate * (1.0 + lax.erf(gate * (2.0 ** -0.5))) * val
        acc = acc + jnp.dot(act.astype(BF16), wd_ref[gc, :], preferred_element_type=F32)
    if final:
        acc = _rms_rows(acc, gf_ref[...])
    o_ref[...] = acc


def _ffn(x, g, wu, cw, cb, wd, gf, T, final):
    n = x.shape[0]
    tm = FFN_TILE
    tpb = T // tm
    hb = tm // HALO
    nhb = n // HALO
    return pl.pallas_call(
        functools.partial(_ffn_kernel, tpb=tpb, final=final),
        grid=(n // tm,),
        in_specs=[
            pl.BlockSpec((HALO, D_MODEL), lambda i: (jnp.maximum(i * hb - 1, 0), 0)),
            pl.BlockSpec((tm, D_MODEL), lambda i: (i, 0)),
            pl.BlockSpec((HALO, D_MODEL), lambda i: (jnp.minimum((i + 1) * hb, nhb - 1), 0)),
            _const_spec((1, D_MODEL)),
            _resident_spec((D_MODEL, 2 * D_FF)),
            _const_spec((3, D_FF)),
            _const_spec((1, D_FF)),
            _resident_spec((D_FF, D_MODEL)),
            _const_spec((1, D_MODEL)),
        ],
        out_specs=pl.BlockSpec((tm, D_MODEL), lambda i: (i, 0)),
        out_shape=jax.ShapeDtypeStruct((n, D_MODEL), F32),
        compiler_params=_cparams("parallel"),
        name="ffn",
    )(x, x, x, g, wu, cw, cb, wd, gf)


def _rope_tables(T):
    t = jnp.arange(T)

    def parts(pos, half):
        freqs = ROPE_THETA ** (-jnp.arange(half, dtype=F32) / half)
        ang = pos.astype(F32)[:, None] * freqs[None, :]
        c, s, z = jnp.cos(ang), jnp.sin(ang), jnp.zeros_like(ang)
        return jnp.concatenate([c, c], -1), jnp.concatenate([-s, z], -1), jnp.concatenate([z, s], -1)

    lin = parts(t, HEAD_DIM // 2)
    row = parts(t // GRID_W, HEAD_DIM // 4)
    colp = parts(t % GRID_W, HEAD_DIM // 4)
    ax = [jnp.concatenate([a, b], -1) for a, b in zip(row, colp)]
    return jnp.concatenate([jnp.tile(z, (1, 2)) for z in (*lin, *ax)], axis=-1)


def _na_bias_table(rpb):
    c = np.arange(GRID_W)
    qstart = np.clip(c - NA_COLS // 2, 0, GRID_W - NA_COLS)
    rel = c[None, :] - qstart[:, None]
    valid = (rel >= 0) & (rel < NA_COLS)
    dc = np.clip(c[None, :] - c[:, None], -(NA_COLS - 1), NA_COLS - 1) + (NA_COLS - 1)
    e = np.arange(NA_ROWS)
    m = np.arange(NA_ROWS)
    dr = m[None, :] - e[:, None] + (NA_ROWS - 1)
    onehot = np.zeros((2 * NA_COLS - 1, GRID_W * GRID_W), np.float32)
    onehot[dc.reshape(-1), np.arange(GRID_W * GRID_W)] = 1.0
    rows = rpb[:, dr].reshape(-1, 2 * NA_COLS - 1)
    tbl = jnp.dot(rows, onehot, precision=lax.Precision.HIGHEST)
    tbl = tbl.reshape(rpb.shape[0], NA_ROWS, NA_ROWS, GRID_W, GRID_W)
    tbl = jnp.where(valid[None, None, None], tbl, NEG)
    tbl = tbl.transpose(1, 0, 3, 2, 4)
    return tbl.reshape(NA_ROWS, 2, 2 * GRID_W, NA_ROWS * GRID_W).astype(F32)


def _gqa_lane_order():
    return np.concatenate([np.arange(HEAD_DIM) + HEAD_DIM * h for h in GQA_HEAD_ORDER])


def _in_column_order():
    cols = np.arange(IN_WIDTH)
    order = _gqa_lane_order()
    cols[0:256] = order
    cols[512:768] = 512 + order
    return cols


def _trunk(x, mem, p):
    B, T, _ = x.shape
    n = B * T
    x = x.reshape(n, D_MODEL)
    tab = _rope_tables(T)
    bd = jnp.asarray(np.kron(np.eye(2), np.ones((HEAD_DIM, HEAD_DIM))), BF16)
    mem2 = mem.reshape(B * MEM_LEN, D_MODEL)
    for l in range(DEPTH):
        qa, ka, va, qb, kb, vb, qc, kc, vc, qc4, kc4, vc4, qc16, kc16, vc16, qd, kd, vd = _proj(
            x, p["norm_mix_g"][l], p["w_in"][l], tab, p["gq"][l], p["gk"][l], bd, T)
        ya = _attn_a(qa, ka, va, B, T)
        yb = _band(qb, kb, vb, nseq=B, L=T, S=BAND_BLOCK, W=BAND_BLOCK + 2 * B_RADIUS, R=B_RADIUS,
                   halo=B_RADIUS, mode="sink", extra=p["sink"][l])
        cs = []
        for (window, d), qkv in zip(C_BRANCHES, ((qc, kc, vc), (qc4, kc4, vc4), (qc16, kc16, vc16))):
            r = window // (2 * d)
            blk = min(C_BLOCK, T // d)
            qs, ks, vs = (z.reshape(n, 256) for z in qkv)
            cs.append(_band(qs, ks, vs, nseq=B * d, L=T // d, S=blk, W=blk + 2 * r, R=r, halo=r, mode="lse"))
        yd = _band(qd, kd, vd, nseq=B, L=T, S=GRID_W, W=NA_ROWS * GRID_W, R=0, halo=(NA_ROWS // 2) * GRID_W,
                   mode="bias", extra=p["na_bias"][l])
        x, qx = _mixout(x, ya, yb, cs[0], cs[1], cs[2], yd, p["grp_g"][l], p["w_out"][l],
                        p["norm_x_g"][l], p["w_xq"][l], T)
        kv = _memkv(mem2, p["norm_mem_g"][l], p["w_xkv"][l])
        x = _cross(qx, kv, x, p["w_xo"][l], T)
        x = _ffn(x, p["norm_ffn_g"][l], p["w_up"][l], p["conv_w"][l], p["conv_b"][l], p["w_down"][l],
                 p["final_g"], T, final=(l == DEPTH - 1))
    return x.reshape(B, T, D_MODEL)


def _prepare(norm_mix_g, w_in, a_q_norm_g, a_k_norm_g, b_sink, d_rpb, grp_norm_g, w_out, norm_x_g,
             norm_mem_g, w_xq, w_xkv, w_xo, norm_ffn_g, w_up, conv_w, conv_b, w_down, final_norm_g):
    order = _gqa_lane_order()
    row_order = np.arange(D_MODEL)
    row_order[0:256] = order
    row_order[256:512] = 256 + order
    grp_g = grp_norm_g.at[:, 0].set(grp_norm_g[:, 0][:, order]).at[:, 1].set(grp_norm_g[:, 1][:, order])
    sink = b_sink[:, np.array(GQA_HEAD_ORDER)].reshape(-1, 2, 2) * LOG2E
    return {
        "norm_mix_g": norm_mix_g[:, None, :],
        "w_in": w_in[:, :, _in_column_order()].astype(BF16),
        "gq": jnp.tile(a_q_norm_g, (1, 2))[:, None, :] * (HEAD_DIM ** -0.5 * LOG2E),
        "gk": jnp.tile(a_k_norm_g, (1, 2))[:, None, :],
        "sink": sink,
        "na_bias": jax.vmap(_na_bias_table)(d_rpb * LOG2E),
        "grp_g": grp_g,
        "w_out": w_out[:, row_order, :].astype(BF16),
        "norm_x_g": norm_x_g[:, None, :],
        "norm_mem_g": norm_mem_g[:, None, :],
        "w_xq": w_xq.astype(BF16),
        "w_xkv": w_xkv.astype(BF16),
        "w_xo": w_xo.astype(BF16),
        "norm_ffn_g": norm_ffn_g[:, None, :],
        "w_up": w_up.astype(BF16),
        "conv_w": conv_w,
        "conv_b": conv_b[:, None, :],
        "w_down": w_down.astype(BF16),
        "final_g": final_norm_g[None, :],
    }


def kernel(x_prompt, x_sample, mem_prompt, mem_sample, norm_mix_g, w_in, a_q_norm_g, a_k_norm_g, b_sink,
           d_rpb, grp_norm_g, w_out, norm_x_g, norm_mem_g, w_xq, w_xkv, w_xo, norm_ffn_g, w_up, conv_w,
           conv_b, w_down, final_norm_g):
    p = _prepare(norm_mix_g, w_in, a_q_norm_g, a_k_norm_g, b_sink, d_rpb, grp_norm_g, w_out, norm_x_g,
                 norm_mem_g, w_xq, w_xkv, w_xo, norm_ffn_g, w_up, conv_w, conv_b, w_down, final_norm_g)
    return _trunk(x_prompt, mem_prompt, p), _trunk(x_sample, mem_sample, p)
```

```python
import functools
import math

import numpy as np
import jax
import jax.numpy as jnp
from jax import lax
from jax.experimental import pallas as pl
from jax.experimental.pallas import tpu as pltpu

F32 = jnp.float32
BF16 = jnp.bfloat16

D_MODEL = 1024
DEPTH = 4
HEAD_DIM = 64
GROUP_WIDTH = 256
GRID_W = 64
ROPE_THETA = 10000.0
B_RADIUS = 128
C_BRANCHES = ((128, 1), (512, 4), (2048, 16))
NA_ROWS = 8
NA_COLS = 16
MEM_LEN = 256
X_HEADS = 4
X_HEAD_DIM = D_MODEL // X_HEADS
D_FF = 2816
EPS = 1e-6
NEG = -1e30
IN_WIDTH = 2560

LANES = 128
VMEM_LIMIT = 56 * 1024 * 1024
TOKEN_TILE = 1024
ROW_CHUNK = TOKEN_TILE
FFN_TILE = 512
BAND_TOKENS = 2048
A_TQ = 512
A_TK = 4096
A_ROWS = 128
BAND_BLOCK = 256
C_BLOCK = 128
BAND_UNROLL_ROWS = 1024
LOG2E = 1.4426950408889634
MXU_COLS = 256
FF_CHUNKS = ((0, 6 * MXU_COLS), (6 * MXU_COLS, D_FF))

GQA_HEAD_ORDER = (0, 2, 1, 3)
NT_DIMS = (((1,), (1,)), ((), ()))


def _cparams(*sem):
    return pltpu.CompilerParams(dimension_semantics=sem, vmem_limit_bytes=VMEM_LIMIT)


def _const_spec(shape):
    n = len(shape)
    return pl.BlockSpec(shape, lambda *_: (0,) * n)


def _resident_spec(shape):
    n = len(shape)
    return pl.BlockSpec(shape, lambda *_: (0,) * n, pipeline_mode=pl.Buffered(1))


def _rms_rows(x, g):
    return x * lax.rsqrt(jnp.mean(x * x, axis=-1, keepdims=True) + EPS) * g


def _low_lanes():
    return lax.broadcasted_iota(jnp.int32, (1, LANES), 1) < HEAD_DIM


def _stack_heads(qb, low):
    zero = jnp.zeros_like(qb)
    return jnp.concatenate([jnp.where(low, qb, zero), jnp.where(low, zero, qb)], axis=0)


def _rope(y, c, sa, sb, shift):
    up = pltpu.roll(y, LANES - shift, 1)
    dn = pltpu.roll(y, shift, 1)
    return y * c + up * sa + dn * sb


def _proj_kernel(x_ref, g_ref, w_ref, tab_ref, gq_ref, gk_ref, bd_ref,
                 qa, ka, va, qb, kb, vb, qc, kc, vc, qc4, kc4, vc4, qc16, kc16, vc16, qd, kd, vd,
                 q_scr, k_scr, v_scr):
    tm = x_ref.shape[0]
    scale = HEAD_DIM ** -0.5 * LOG2E

    def halves(y):
        return y[:, :LANES], y[:, LANES:]

    def qknorm(y, g):
        ss = jnp.dot((y * y).astype(BF16), bd_ref[...], preferred_element_type=F32) * (1.0 / HEAD_DIM)
        return y * lax.rsqrt(ss + EPS) * g

    for rc in range(tm // ROW_CHUNK):
        r0 = rc * ROW_CHUNK
        rows = slice(r0, r0 + ROW_CHUNK)
        h = _rms_rows(x_ref[rows, :], g_ref[...]).astype(BF16)

        def mm(c0):
            return jnp.dot(h, w_ref[:, c0:c0 + 256], preferred_element_type=F32)

        def tab(k):
            return tab_ref[rows, k * LANES:(k + 1) * LANES]

        def rope_lin(y):
            return _rope(y, tab(0), tab(1), tab(2), HEAD_DIM // 2)

        def rope_ax(y):
            return _rope(y, tab(3), tab(4), tab(5), HEAD_DIM // 4)

        def emit_c(scr, y0, y1, nat, subs):
            for hb, y in enumerate((y0, y1)):
                cols = slice(hb * LANES, (hb + 1) * LANES)
                scr[rc, hb] = y
                nat[rows, cols] = y.astype(BF16)
                for (_, d), ref in zip(C_BRANCHES[1:], subs):
                    for r in range(d):
                        ref[0, r, r0 // d:(r0 + ROW_CHUNK) // d, cols] = (
                            scr[rc, hb, pl.ds(r, ROW_CHUNK // d, stride=d), :].astype(BF16))

        y0, y1 = halves(mm(0))
        qa[rows, :LANES] = rope_ax(qknorm(y0, gq_ref[...])).astype(BF16)
        qa[rows, LANES:] = rope_ax(qknorm(y1, gq_ref[...])).astype(BF16)
        y0, y1 = halves(mm(256))
        ka[rows, :] = rope_ax(qknorm(y0, gk_ref[...])).astype(BF16)
        va[rows, :] = y1.astype(BF16)

        y0, y1 = halves(mm(512))
        qb[rows, :LANES] = rope_lin(y0 * scale).astype(BF16)
        qb[rows, LANES:] = rope_lin(y1 * scale).astype(BF16)
        y0, y1 = halves(mm(768))
        kb[rows, :] = rope_lin(y0).astype(BF16)
        vb[rows, :] = y1.astype(BF16)

        y0, y1 = halves(mm(1024))
        emit_c(q_scr, rope_lin(y0 * scale), rope_lin(y1 * scale), qc, (qc4, qc16))
        y0, y1 = halves(mm(1280))
        emit_c(k_scr, rope_lin(y0), rope_lin(y1), kc, (kc4, kc16))
        y0, y1 = halves(mm(1536))
        emit_c(v_scr, y0, y1, vc, (vc4, vc16))

        qd[rows, :] = (mm(1792) * scale).astype(BF16)
        kd[rows, :] = mm(2048).astype(BF16)
        vd[rows, :] = mm(2304).astype(BF16)


def _proj(x, g, w, tab, gq, gk, bd, T):
    n = x.shape[0]
    tm = TOKEN_TILE
    tpb = T // tm
    B = n // T
    row = lambda wd: (pl.BlockSpec((tm, wd), lambda i: (i, 0)), jax.ShapeDtypeStruct((n, wd), BF16))

    def slab(d):
        return (pl.BlockSpec((1, d, tm // d, 256), lambda i: (i // tpb, 0, i % tpb, 0)),
                jax.ShapeDtypeStruct((B, d, T // d, 256), BF16))

    outs = ([row(256), row(128), row(128)] * 2 + [row(256)] * 3
            + [slab(d) for _, d in C_BRANCHES[1:] for _ in range(3)] + [row(256)] * 3)
    return pl.pallas_call(
        _proj_kernel,
        grid=(n // tm,),
        in_specs=[
            pl.BlockSpec((tm, D_MODEL), lambda i: (i, 0)),
            _const_spec((1, D_MODEL)),
            _resident_spec((D_MODEL, IN_WIDTH)),
            pl.BlockSpec((tm, 6 * LANES), lambda i: (i % tpb, 0)),
            _const_spec((1, LANES)),
            _const_spec((1, LANES)),
            _const_spec((LANES, LANES)),
        ],
        out_specs=[o[0] for o in outs],
        out_shape=[o[1] for o in outs],
        scratch_shapes=[pltpu.VMEM((tm // ROW_CHUNK, 2, ROW_CHUNK, LANES), F32)] * 3,
        compiler_params=_cparams("parallel"),
        name="proj",
    )(x, g, w, tab, gq, gk, bd)


def _attn_a_kernel(q_ref, k_ref, v_ref, o_ref, m_sc, acc_sc, *, nk):
    j = pl.program_id(2)
    tq = q_ref.shape[0]
    tk = k_ref.shape[0]

    @pl.when(j == 0)
    def _():
        m_sc[...] = jnp.full(m_sc.shape, NEG, F32)
        acc_sc[...] = jnp.zeros(acc_sc.shape, F32)

    low = _low_lanes()
    k = k_ref[...]
    v1 = jnp.concatenate([v_ref[...], jnp.ones((tk, LANES), BF16)], axis=1)
    for jb in range(2):
        q2 = _stack_heads(q_ref[:, jb * LANES:(jb + 1) * LANES], low)
        for rb in range(2 * tq // A_ROWS):
            rows = slice(rb * A_ROWS, (rb + 1) * A_ROWS)
            s = lax.dot_general(q2[rows], k, NT_DIMS, preferred_element_type=F32)
            m = m_sc[jb, rows, :]
            m_new = jnp.maximum(m, jnp.max(s, axis=-1, keepdims=True))
            p = jnp.exp2(s - m_new).astype(BF16)
            acc_sc[jb, rows, :] = (acc_sc[jb, rows, :] * jnp.exp2(m - m_new)
                                   + jnp.dot(p, v1, preferred_element_type=F32))
            m_sc[jb, rows, :] = m_new

    @pl.when(j == nk - 1)
    def _():
        for jb in range(2):
            cols = slice(jb * LANES, (jb + 1) * LANES)
            acc = acc_sc[jb]
            o2 = acc[:, :LANES] / acc[:, LANES:]
            o_ref[:, cols] = jnp.where(low, o2[:tq], o2[tq:]).astype(o_ref.dtype)


def _attn_a(q, k, v, B, T):
    n = q.shape[0]
    tq = A_TQ
    tk = min(A_TK, T)
    nq, nk = T // tq, T // tk
    return pl.pallas_call(
        functools.partial(_attn_a_kernel, nk=nk),
        grid=(B, nq, nk),
        in_specs=[
            pl.BlockSpec((tq, 256), lambda b, i, j: (b * nq + i, 0)),
            pl.BlockSpec((tk, LANES), lambda b, i, j: (b * nk + j, 0)),
            pl.BlockSpec((tk, LANES), lambda b, i, j: (b * nk + j, 0)),
        ],
        out_specs=pl.BlockSpec((tq, 256), lambda b, i, j: (b * nq + i, 0)),
        out_shape=jax.ShapeDtypeStruct((n, 256), BF16),
        scratch_shapes=[
            pltpu.VMEM((2, 2 * tq, 1), F32),
            pltpu.VMEM((2, 2 * tq, 256), F32),
        ],
        compiler_params=_cparams("parallel", "parallel", "arbitrary"),
        name="attn_a",
    )(q, k, v)


def _band_kernel(*refs, S, W, halo, align, L, kvw, mode, unroll):
    if mode == "sink":
        sink_ref, q_ref, k_ref, v_ref, tab_ref, o_ref = refs
    elif mode == "lse":
        q_ref, k_ref, v_ref, tab_ref, o_ref, lse_ref = refs
    else:
        q_ref, k_ref, v_ref, tab_ref, o_ref = refs
    qi = pl.program_id(1)
    G, TQ = q_ref.shape[0], q_ref.shape[1]
    per = TQ // S
    low = _low_lanes()
    ones = jnp.ones((W, LANES), BF16)
    first_half = lax.broadcasted_iota(jnp.int32, (2 * S, 1), 0) < S

    def body(it, carry):
        g = it // per
        i = it % per
        ig = qi * per + i
        start = pl.multiple_of(jnp.clip(ig * S - halo, 0, L - W), align)
        q0 = pl.multiple_of(i * S, S)
        qs = q_ref[g, pl.ds(q0, S), :]
        ks = k_ref[g, pl.ds(start, W), :]
        vs = v_ref[g, pl.ds(start, W), :]
        case = (ig * S - start) // (S if mode == "bias" else halo)
        for jb in range(2):
            cols = slice(jb * LANES, (jb + 1) * LANES)
            kcols = cols if kvw == 256 else slice(0, LANES)
            q2 = _stack_heads(qs[:, cols], low)
            s = lax.dot_general(q2, ks[:, kcols], NT_DIMS, preferred_element_type=F32)
            s = s + (tab_ref[case, jb] if mode == "bias" else tab_ref[case])
            m = jnp.max(s, axis=-1, keepdims=True)
            if mode == "sink":
                sk = jnp.where(first_half, sink_ref[jb, 0], sink_ref[jb, 1])
                m = jnp.maximum(m, sk)
            p = jnp.exp2(s - m).astype(BF16)
            v1 = jnp.concatenate([vs[:, kcols], ones], axis=1)
            pv = jnp.dot(p, v1, preferred_element_type=F32)
            l = pv[:, LANES:]
            if mode == "sink":
                l = l + jnp.exp2(sk - m)
            num = jnp.where(low, pv[:S, :LANES], pv[S:, :LANES])
            den = jnp.where(low, l[:S], l[S:])
            o_ref[g, pl.ds(q0, S), cols] = (num / den).astype(o_ref.dtype)
            if mode == "lse":
                lse_ref[g, pl.ds(q0, S), cols] = (jnp.where(low, m[:S], m[S:]) + jnp.log2(den)) * (1.0 / LOG2E)
        return carry

    lax.fori_loop(0, G * per, body, 0, unroll=unroll)


def _band_mask_table(S, W, R, halo):
    rowq = np.arange(2 * S) % S
    col = np.arange(W)
    tabs = [np.where(np.abs(col[None, :] - rowq[:, None] - c * halo) <= R, 0.0, NEG) for c in range(3)]
    return jnp.asarray(np.stack(tabs), F32)


def _band(q, k, v, *, nseq, L, S, W, R, halo, mode, extra=None):
    kvw = k.shape[1]
    W = min(W, L)
    TQ = min(L, BAND_TOKENS)
    G = max(1, BAND_TOKENS // L)
    S = min(S, L)
    align = math.gcd(S, halo, L - W) if L > W else S
    unroll = max(1, BAND_UNROLL_ROWS // S)
    assert nseq % G == 0 and L % TQ == 0 and TQ % S == 0 and align % 16 == 0
    q3 = q.reshape(nseq, L, 256)
    k3 = k.reshape(nseq, L, kvw)
    v3 = v.reshape(nseq, L, kvw)
    qspec = pl.BlockSpec((G, TQ, 256), lambda s, i: (s, i, 0))
    kvspec = pl.BlockSpec((G, L, kvw), lambda s, i: (s, 0, 0))
    in_specs = [qspec, kvspec, kvspec]
    args = [q3, k3, v3]
    out_specs = qspec
    out_shape = jax.ShapeDtypeStruct((nseq, L, 256), BF16)
    tab = extra if mode == "bias" else _band_mask_table(S, W, R, halo)
    in_specs = in_specs + [_resident_spec(tab.shape)]
    args = args + [tab]
    if mode == "sink":
        in_specs = [pl.BlockSpec(memory_space=pltpu.SMEM)] + in_specs
        args = [extra] + args
    elif mode == "lse":
        out_specs = [qspec, qspec]
        out_shape = [out_shape, jax.ShapeDtypeStruct((nseq, L, 256), F32)]
    out = pl.pallas_call(
        functools.partial(_band_kernel, S=S, W=W, halo=halo, align=align, L=L, kvw=kvw, mode=mode,
                          unroll=unroll),
        grid=(nseq // G, L // TQ),
        in_specs=in_specs,
        out_specs=out_specs,
        out_shape=out_shape,
        compiler_params=_cparams("parallel", "arbitrary"),
        name="band_" + mode,
    )(*args)
    if mode == "lse":
        return out[0].reshape(nseq * L, 256), out[1].reshape(nseq * L, 256)
    return out.reshape(nseq * L, 256)


def _mixout_kernel(x_ref, ya_ref, yb_ref, c1o, c1l, c2o, c2l, c3o, c3l, yd_ref,
                   gg_ref, wo_ref, gx_ref, wq_ref, xo_ref, qx_ref, *scr):
    tm = x_ref.shape[0]

    def gn(y, gi):
        return _rms_rows(y.astype(F32), gg_ref[gi:gi + 1, :]).astype(BF16)

    def token_order(ref, buf):
        d = ref.shape[1]
        for hb in range(2):
            for r in range(d):
                buf[hb, pl.ds(r, tm // d, stride=d), :] = ref[0, r, :, hb * LANES:(hb + 1) * LANES].astype(F32)
        return jnp.concatenate([buf[0], buf[1]], axis=-1)

    o1, l1 = c1o[...].astype(F32), c1l[...]
    o2, l2 = token_order(c2o, scr[0]), token_order(c2l, scr[1])
    o3, l3 = token_order(c3o, scr[2]), token_order(c3l, scr[3])
    mx = jnp.maximum(jnp.maximum(l1, l2), l3)
    e1, e2, e3 = jnp.exp(l1 - mx), jnp.exp(l2 - mx), jnp.exp(l3 - mx)
    yc = (e1 * o1 + e2 * o2 + e3 * o3) / (e1 + e2 + e3)
    ycat = jnp.concatenate([gn(ya_ref[...], 0), gn(yb_ref[...], 1), gn(yc, 2), gn(yd_ref[...], 3)], axis=-1)
    xn = x_ref[...] + jnp.dot(ycat, wo_ref[...], preferred_element_type=F32)
    xo_ref[...] = xn
    hq = _rms_rows(xn, gx_ref[...]).astype(BF16)
    qx_ref[...] = (jnp.dot(hq, wq_ref[...], preferred_element_type=F32) * (X_HEAD_DIM ** -0.5)).astype(BF16)


def _mixout(x, ya, yb, c1, c2, c3, yd, gg, wo, gx, wq, T):
    n = x.shape[0]
    B = n // T
    tm = TOKEN_TILE
    tpb = T // tm
    row = lambda wd: pl.BlockSpec((tm, wd), lambda i: (i, 0))
    slab = lambda d: pl.BlockSpec((1, d, tm // d, 256), lambda i: (i // tpb, 0, i % tpb, 0))
    d2, d3 = C_BRANCHES[1][1], C_BRANCHES[2][1]
    return pl.pallas_call(
        _mixout_kernel,
        grid=(n // tm,),
        in_specs=[row(D_MODEL)] + [row(256)] * 4 + [slab(d2)] * 2 + [slab(d3)] * 2 + [row(256)] + [
            _const_spec((4, GROUP_WIDTH)),
            _resident_spec((D_MODEL, D_MODEL)),
            _const_spec((1, D_MODEL)),
            _resident_spec((D_MODEL, D_MODEL)),
        ],
        out_specs=[row(D_MODEL), row(D_MODEL)],
        out_shape=[jax.ShapeDtypeStruct((n, D_MODEL), F32), jax.ShapeDtypeStruct((n, D_MODEL), BF16)],
        scratch_shapes=[pltpu.VMEM((2, tm, LANES), F32)] * 4,
        compiler_params=_cparams("parallel"),
        name="mixout",
    )(x, ya, yb, c1[0], c1[1], *[z.reshape(B, d2, T // d2, 256) for z in c2],
      *[z.reshape(B, d3, T // d3, 256) for z in c3], yd, gg, wo, gx, wq)


def _memkv_kernel(m_ref, g_ref, w_ref, kv_ref):
    h = _rms_rows(m_ref[...], g_ref[...]).astype(BF16)
    kv_ref[...] = jnp.dot(h, w_ref[...], preferred_element_type=F32).astype(BF16)


def _memkv(mem, g, w):
    n = mem.shape[0]
    return pl.pallas_call(
        _memkv_kernel,
        grid=(n // MEM_LEN,),
        in_specs=[
            pl.BlockSpec((MEM_LEN, D_MODEL), lambda i: (i, 0)),
            _const_spec((1, D_MODEL)),
            _resident_spec((D_MODEL, 2 * D_MODEL)),
        ],
        out_specs=pl.BlockSpec((MEM_LEN, 2 * D_MODEL), lambda i: (i, 0)),
        out_shape=jax.ShapeDtypeStruct((n, 2 * D_MODEL), BF16),
        compiler_params=_cparams("parallel"),
        name="memkv",
    )(mem, g, w)


def _cross_kernel(qx_ref, kv_ref, x_ref, wo_ref, xo_ref):
    outs = []
    for h in range(X_HEADS):
        cols = slice(h * X_HEAD_DIM, (h + 1) * X_HEAD_DIM)
        vcols = slice(D_MODEL + h * X_HEAD_DIM, D_MODEL + (h + 1) * X_HEAD_DIM)
        s = lax.dot_general(qx_ref[:, cols], kv_ref[:, cols], NT_DIMS, preferred_element_type=F32)
        m = jnp.max(s, axis=-1, keepdims=True)
        p = jnp.exp(s - m)
        l = jnp.sum(p, axis=-1, keepdims=True)
        o = jnp.dot(p.astype(BF16), kv_ref[:, vcols], preferred_element_type=F32) / l
        outs.append(o.astype(BF16))
    o = jnp.concatenate(outs, axis=-1)
    xo_ref[...] = x_ref[...] + jnp.dot(o, wo_ref[...], preferred_element_type=F32)


def _cross(qx, kv, x, wo, T):
    n = x.shape[0]
    tm = TOKEN_TILE
    tpb = T // tm
    row = lambda wd: pl.BlockSpec((tm, wd), lambda i: (i, 0))
    return pl.pallas_call(
        _cross_kernel,
        grid=(n // tm,),
        in_specs=[
            row(D_MODEL),
            pl.BlockSpec((MEM_LEN, 2 * D_MODEL), lambda i: (i // tpb, 0)),
            row(D_MODEL),
            _resident_spec((D_MODEL, D_MODEL)),
        ],
        out_specs=row(D_MODEL),
        out_shape=jax.ShapeDtypeStruct((n, D_MODEL), F32),
        compiler_params=_cparams("parallel"),
        name="cross",
    )(qx, kv, x, wo)


HALO = 16


def _ffn_kernel(xp_ref, x_ref, xn_ref, g_ref, wu_ref, cw_ref, cb_ref, wd_ref, gf_ref, o_ref, *, tpb, final):
    i = pl.program_id(0)
    tm = x_ref.shape[0]
    x = x_ref[...]
    xe = jnp.concatenate([xp_ref[...], x, xn_ref[...]], axis=0)
    he = _rms_rows(xe, g_ref[...]).astype(BF16)
    hc = he[HALO:HALO + tm]
    rows = lax.broadcasted_iota(jnp.int32, (tm + 2 * HALO, 1), 0)
    seq_first = (i % tpb) == 0
    seq_last = (i % tpb) == tpb - 1
    keep = jnp.logical_and(jnp.logical_or(rows >= HALO, jnp.logical_not(seq_first)),
                           jnp.logical_or(rows < HALO + tm, jnp.logical_not(seq_last)))
    keep = keep.astype(F32)
    acc = x
    for c0, c1 in FF_CHUNKS:
        gc = slice(c0, c1)
        uc = slice(D_FF + c0, D_FF + c1)
        ge = jnp.dot(he, wu_ref[:, gc], preferred_element_type=F32) * keep
        val = jnp.dot(hc, wu_ref[:, uc], preferred_element_type=F32)
        gate = (ge[HALO - 1:HALO - 1 + tm] * cw_ref[0:1, gc] + ge[HALO:HALO + tm] * cw_ref[1:2, gc]
                + ge[HALO + 1:HALO + 1 + tm] * cw_ref[2:3, gc] + cb_ref[:, gc])
        act = 0.5 * gate * (1.0 + lax.erf(gate * (2.0 ** -0.5))) * val
        acc = acc + jnp.dot(act.astype(BF16), wd_ref[gc, :], preferred_element_type=F32)
    if final:
        acc = _rms_rows(acc, gf_ref[...])
    o_ref[...] = acc


def _ffn(x, g, wu, cw, cb, wd, gf, T, final):
    n = x.shape[0]
    tm = FFN_TILE
    tpb = T // tm
    hb = tm // HALO
    nhb = n // HALO
    return pl.pallas_call(
        functools.partial(_ffn_kernel, tpb=tpb, final=final),
        grid=(n // tm,),
        in_specs=[
            pl.BlockSpec((HALO, D_MODEL), lambda i: (jnp.maximum(i * hb - 1, 0), 0)),
            pl.BlockSpec((tm, D_MODEL), lambda i: (i, 0)),
            pl.BlockSpec((HALO, D_MODEL), lambda i: (jnp.minimum((i + 1) * hb, nhb - 1), 0)),
            _const_spec((1, D_MODEL)),
            _resident_spec((D_MODEL, 2 * D_FF)),
            _const_spec((3, D_FF)),
            _const_spec((1, D_FF)),
            _resident_spec((D_FF, D_MODEL)),
            _const_spec((1, D_MODEL)),
        ],
        out_specs=pl.BlockSpec((tm, D_MODEL), lambda i: (i, 0)),
        out_shape=jax.ShapeDtypeStruct((n, D_MODEL), F32),
        compiler_params=_cparams("parallel"),
        name="ffn",
    )(x, x, x, g, wu, cw, cb, wd, gf)


def _rope_tables(T):
    t = jnp.arange(T)

    def parts(pos, half):
        freqs = ROPE_THETA ** (-jnp.arange(half, dtype=F32) / half)
        ang = pos.astype(F32)[:, None] * freqs[None, :]
        c, s, z = jnp.cos(ang), jnp.sin(ang), jnp.zeros_like(ang)
        return jnp.concatenate([c, c], -1), jnp.concatenate([-s, z], -1), jnp.concatenate([z, s], -1)

    lin = parts(t, HEAD_DIM // 2)
    row = parts(t // GRID_W, HEAD_DIM // 4)
    colp = parts(t % GRID_W, HEAD_DIM // 4)
    ax = [jnp.concatenate([a, b], -1) for a, b in zip(row, colp)]
    return jnp.concatenate([jnp.tile(z, (1, 2)) for z in (*lin, *ax)], axis=-1)


def _na_bias_table(rpb):
    c = np.arange(GRID_W)
    qstart = np.clip(c - NA_COLS // 2, 0, GRID_W - NA_COLS)
    rel = c[None, :] - qstart[:, None]
    valid = (rel >= 0) & (rel < NA_COLS)
    dc = np.clip(c[None, :] - c[:, None], -(NA_COLS - 1), NA_COLS - 1) + (NA_COLS - 1)
    e = np.arange(NA_ROWS)
    m = np.arange(NA_ROWS)
    dr = m[None, :] - e[:, None] + (NA_ROWS - 1)
    onehot = np.zeros((2 * NA_COLS - 1, GRID_W * GRID_W), np.float32)
    onehot[dc.reshape(-1), np.arange(GRID_W * GRID_W)] = 1.0
    rows = rpb[:, dr].reshape(-1, 2 * NA_COLS - 1)
    tbl = jnp.dot(rows, onehot, precision=lax.Precision.HIGHEST)
    tbl = tbl.reshape(rpb.shape[0], NA_ROWS, NA_ROWS, GRID_W, GRID_W)
    tbl = jnp.where(valid[None, None, None], tbl, NEG)
    tbl = tbl.transpose(1, 0, 3, 2, 4)
    return tbl.reshape(NA_ROWS, 2, 2 * GRID_W, NA_ROWS * GRID_W).astype(F32)


def _gqa_lane_order():
    return np.concatenate([np.arange(HEAD_DIM) + HEAD_DIM * h for h in GQA_HEAD_ORDER])


def _in_column_order():
    cols = np.arange(IN_WIDTH)
    order = _gqa_lane_order()
    cols[0:256] = order
    cols[512:768] = 512 + order
    return cols


def _trunk(x, mem, p):
    B, T, _ = x.shape
    n = B * T
    x = x.reshape(n, D_MODEL)
    tab = _rope_tables(T)
    bd = jnp.asarray(np.kron(np.eye(2), np.ones((HEAD_DIM, HEAD_DIM))), BF16)
    mem2 = mem.reshape(B * MEM_LEN, D_MODEL)
    for l in range(DEPTH):
        qa, ka, va, qb, kb, vb, qc, kc, vc, qc4, kc4, vc4, qc16, kc16, vc16, qd, kd, vd = _proj(
            x, p["norm_mix_g"][l], p["w_in"][l], tab, p["gq"][l], p["gk"][l], bd, T)
        ya = _attn_a(qa, ka, va, B, T)
        yb = _band(qb, kb, vb, nseq=B, L=T, S=BAND_BLOCK, W=BAND_BLOCK + 2 * B_RADIUS, R=B_RADIUS,
                   halo=B_RADIUS, mode="sink", extra=p["sink"][l])
        cs = []
        for (window, d), qkv in zip(C_BRANCHES, ((qc, kc, vc), (qc4, kc4, vc4), (qc16, kc16, vc16))):
            r = window // (2 * d)
            blk = min(C_BLOCK, T // d)
            qs, ks, vs = (z.reshape(n, 256) for z in qkv)
            cs.append(_band(qs, ks, vs, nseq=B * d, L=T // d, S=blk, W=blk + 2 * r, R=r, halo=r, mode="lse"))
        yd = _band(qd, kd, vd, nseq=B, L=T, S=GRID_W, W=NA_ROWS * GRID_W, R=0, halo=(NA_ROWS // 2) * GRID_W,
                   mode="bias", extra=p["na_bias"][l])
        x, qx = _mixout(x, ya, yb, cs[0], cs[1], cs[2], yd, p["grp_g"][l], p["w_out"][l],
                        p["norm_x_g"][l], p["w_xq"][l], T)
        kv = _memkv(mem2, p["norm_mem_g"][l], p["w_xkv"][l])
        x = _cross(qx, kv, x, p["w_xo"][l], T)
        x = _ffn(x, p["norm_ffn_g"][l], p["w_up"][l], p["conv_w"][l], p["conv_b"][l], p["w_down"][l],
                 p["final_g"], T, final=(l == DEPTH - 1))
    return x.reshape(B, T, D_MODEL)


def _prepare(norm_mix_g, w_in, a_q_norm_g, a_k_norm_g, b_sink, d_rpb, grp_norm_g, w_out, norm_x_g,
             norm_mem_g, w_xq, w_xkv, w_xo, norm_ffn_g, w_up, conv_w, conv_b, w_down, final_norm_g):
    order = _gqa_lane_order()
    row_order = np.arange(D_MODEL)
    row_order[0:256] = order
    row_order[256:512] = 256 + order
    grp_g = grp_norm_g.at[:, 0].set(grp_norm_g[:, 0][:, order]).at[:, 1].set(grp_norm_g[:, 1][:, order])
    sink = b_sink[:, np.array(GQA_HEAD_ORDER)].reshape(-1, 2, 2) * LOG2E
    return {
        "norm_mix_g": norm_mix_g[:, None, :],
        "w_in": w_in[:, :, _in_column_order()].astype(BF16),
        "gq": jnp.tile(a_q_norm_g, (1, 2))[:, None, :] * (HEAD_DIM ** -0.5 * LOG2E),
        "gk": jnp.tile(a_k_norm_g, (1, 2))[:, None, :],
        "sink": sink,
        "na_bias": jax.vmap(_na_bias_table)(d_rpb * LOG2E),
        "grp_g": grp_g,
        "w_out": w_out[:, row_order, :].astype(BF16),
        "norm_x_g": norm_x_g[:, None, :],
        "norm_mem_g": norm_mem_g[:, None, :],
        "w_xq": w_xq.astype(BF16),
        "w_xkv": w_xkv.astype(BF16),
        "w_xo": w_xo.astype(BF16),
        "norm_ffn_g": norm_ffn_g[:, None, :],
        "w_up": w_up.astype(BF16),
        "conv_w": conv_w,
        "conv_b": conv_b[:, None, :],
        "w_down": w_down.astype(BF16),
        "final_g": final_norm_g[None, :],
    }


def kernel(x_prompt, x_sample, mem_prompt, mem_sample, norm_mix_g, w_in, a_q_norm_g, a_k_norm_g, b_sink,
           d_rpb, grp_norm_g, w_out, norm_x_g, norm_mem_g, w_xq, w_xkv, w_xo, norm_ffn_g, w_up, conv_w,
           conv_b, w_down, final_norm_g):
    p = _prepare(norm_mix_g, w_in, a_q_norm_g, a_k_norm_g, b_sink, d_rpb, grp_norm_g, w_out, norm_x_g,
                 norm_mem_g, w_xq, w_xkv, w_xo, norm_ffn_g, w_up, conv_w, conv_b, w_down, final_norm_g)
    return _trunk(x_prompt, mem_prompt, p), _trunk(x_sample, mem_sample, p)
```

```python
import functools
import math

import numpy as np
import jax
import jax.numpy as jnp
from jax import lax
from jax.experimental import pallas as pl
from jax.experimental.pallas import tpu as pltpu

F32 = jnp.float32
BF16 = jnp.bfloat16

D_MODEL = 1024
DEPTH = 4
HEAD_DIM = 64
GROUP_WIDTH = 256
GRID_W = 64
ROPE_THETA = 10000.0
B_RADIUS = 128
C_BRANCHES = ((128, 1), (512, 4), (2048, 16))
NA_ROWS = 8
NA_COLS = 16
MEM_LEN = 256
X_HEADS = 4
X_HEAD_DIM = D_MODEL // X_HEADS
D_FF = 2816
EPS = 1e-6
NEG = -1e30
IN_WIDTH = 2560

LANES = 128
VMEM_LIMIT = 56 * 1024 * 1024
TOKEN_TILE = 1024
FFN_TILE = 512
BAND_TOKENS = 2048
A_TQ = 512
A_TK = 4096
A_ROWS = 128
BAND_BLOCK = 256
C_BLOCK = 128
BAND_UNROLL_ROWS = 1024
LOG2E = 1.4426950408889634
MXU_COLS = 256
FF_CHUNKS = ((0, 6 * MXU_COLS), (6 * MXU_COLS, D_FF))

GQA_HEAD_ORDER = (0, 2, 1, 3)
NT_DIMS = (((1,), (1,)), ((), ()))


def _cparams(*sem):
    return pltpu.CompilerParams(dimension_semantics=sem, vmem_limit_bytes=VMEM_LIMIT)


def _const_spec(shape):
    n = len(shape)
    return pl.BlockSpec(shape, lambda *_: (0,) * n)


def _resident_spec(shape):
    n = len(shape)
    return pl.BlockSpec(shape, lambda *_: (0,) * n, pipeline_mode=pl.Buffered(1))


def _rms_rows(x, g):
    return x * lax.rsqrt(jnp.mean(x * x, axis=-1, keepdims=True) + EPS) * g


def _low_lanes():
    return lax.broadcasted_iota(jnp.int32, (1, LANES), 1) < HEAD_DIM


def _stack_heads(qb, low):
    zero = jnp.zeros_like(qb)
    return jnp.concatenate([jnp.where(low, qb, zero), jnp.where(low, zero, qb)], axis=0)


def _rope(y, c, sa, sb, shift):
    up = pltpu.roll(y, LANES - shift, 1)
    dn = pltpu.roll(y, shift, 1)
    return y * c + up * sa + dn * sb


def _proj_kernel(x_ref, g_ref, w_ref, tab_ref, gq_ref, gk_ref, bd_ref,
                 qa, ka, va, qb, kb, vb, qc, kc, vc, qc4, kc4, vc4, qc16, kc16, vc16, qd, kd, vd,
                 q_scr, k_scr, v_scr):
    tm = x_ref.shape[0]
    scale = HEAD_DIM ** -0.5 * LOG2E
    h = _rms_rows(x_ref[...], g_ref[...]).astype(BF16)

    def halves(y):
        return y[:, :LANES], y[:, LANES:]

    def tab(k):
        return tab_ref[:, k * LANES:(k + 1) * LANES]

    def rope_lin(y):
        return _rope(y, tab(0), tab(1), tab(2), HEAD_DIM // 2)

    def rope_ax(y):
        return _rope(y, tab(3), tab(4), tab(5), HEAD_DIM // 4)

    def qknorm(y, g):
        ss = jnp.dot((y * y).astype(BF16), bd_ref[...], preferred_element_type=F32) * (1.0 / HEAD_DIM)
        return y * lax.rsqrt(ss + EPS) * g

    def emit_c(scr, y0, y1, nat, subs):
        for hb, y in enumerate((y0, y1)):
            cols = slice(hb * LANES, (hb + 1) * LANES)
            scr[hb] = y
            nat[:, cols] = y.astype(BF16)
            for (_, d), ref in zip(C_BRANCHES[1:], subs):
                for r in range(d):
                    ref[0, r, :, cols] = scr[hb, pl.ds(r, tm // d, stride=d), :].astype(BF16)

    ys = [halves(jnp.dot(h, w_ref[:, c0:c0 + 256], preferred_element_type=F32))
          for c0 in range(0, IN_WIDTH, 256)]

    y0, y1 = ys[0]
    qa[:, :LANES] = rope_ax(qknorm(y0, gq_ref[...])).astype(BF16)
    qa[:, LANES:] = rope_ax(qknorm(y1, gq_ref[...])).astype(BF16)
    y0, y1 = ys[1]
    ka[...] = rope_ax(qknorm(y0, gk_ref[...])).astype(BF16)
    va[...] = y1.astype(BF16)

    y0, y1 = ys[2]
    qb[:, :LANES] = rope_lin(y0 * scale).astype(BF16)
    qb[:, LANES:] = rope_lin(y1 * scale).astype(BF16)
    y0, y1 = ys[3]
    kb[...] = rope_lin(y0).astype(BF16)
    vb[...] = y1.astype(BF16)

    y0, y1 = ys[4]
    emit_c(q_scr, rope_lin(y0 * scale), rope_lin(y1 * scale), qc, (qc4, qc16))
    y0, y1 = ys[5]
    emit_c(k_scr, rope_lin(y0), rope_lin(y1), kc, (kc4, kc16))
    y0, y1 = ys[6]
    emit_c(v_scr, y0, y1, vc, (vc4, vc16))

    y0, y1 = ys[7]
    qd[:, :LANES] = (y0 * scale).astype(BF16)
    qd[:, LANES:] = (y1 * scale).astype(BF16)
    for (y0, y1), ref in zip(ys[8:], (kd, vd)):
        ref[:, :LANES] = y0.astype(BF16)
        ref[:, LANES:] = y1.astype(BF16)


def _proj(x, g, w, tab, gq, gk, bd, T):
    n = x.shape[0]
    tm = TOKEN_TILE
    tpb = T // tm
    B = n // T
    row = lambda wd: (pl.BlockSpec((tm, wd), lambda i: (i, 0)), jax.ShapeDtypeStruct((n, wd), BF16))

    def slab(d):
        return (pl.BlockSpec((1, d, tm // d, 256), lambda i: (i // tpb, 0, i % tpb, 0)),
                jax.ShapeDtypeStruct((B, d, T // d, 256), BF16))

    outs = ([row(256), row(128), row(128)] * 2 + [row(256)] * 3
            + [slab(d) for _, d in C_BRANCHES[1:] for _ in range(3)] + [row(256)] * 3)
    return pl.pallas_call(
        _proj_kernel,
        grid=(n // tm,),
        in_specs=[
            pl.BlockSpec((tm, D_MODEL), lambda i: (i, 0)),
            _const_spec((1, D_MODEL)),
            _resident_spec((D_MODEL, IN_WIDTH)),
            pl.BlockSpec((tm, 6 * LANES), lambda i: (i % tpb, 0)),
            _const_spec((1, LANES)),
            _const_spec((1, LANES)),
            _const_spec((LANES, LANES)),
        ],
        out_specs=[o[0] for o in outs],
        out_shape=[o[1] for o in outs],
        scratch_shapes=[pltpu.VMEM((2, tm, LANES), F32)] * 3,
        compiler_params=_cparams("parallel"),
        name="proj",
    )(x, g, w, tab, gq, gk, bd)


def _attn_a_kernel(q_ref, k_ref, v_ref, o_ref, m_sc, acc_sc, *, nk):
    j = pl.program_id(2)
    tq = q_ref.shape[0]
    tk = k_ref.shape[0]

    @pl.when(j == 0)
    def _():
        m_sc[...] = jnp.full(m_sc.shape, NEG, F32)
        acc_sc[...] = jnp.zeros(acc_sc.shape, F32)

    low = _low_lanes()
    k = k_ref[...]
    v1 = jnp.concatenate([v_ref[...], jnp.ones((tk, LANES), BF16)], axis=1)
    for jb in range(2):
        q2 = _stack_heads(q_ref[:, jb * LANES:(jb + 1) * LANES], low)
        for rb in range(2 * tq // A_ROWS):
            rows = slice(rb * A_ROWS, (rb + 1) * A_ROWS)
            s = lax.dot_general(q2[rows], k, NT_DIMS, preferred_element_type=F32)
            m = m_sc[jb, rows, :]
            m_new = jnp.maximum(m, jnp.max(s, axis=-1, keepdims=True))
            p = jnp.exp2(s - m_new).astype(BF16)
            acc_sc[jb, rows, :] = (acc_sc[jb, rows, :] * jnp.exp2(m - m_new)
                                   + jnp.dot(p, v1, preferred_element_type=F32))
            m_sc[jb, rows, :] = m_new

    @pl.when(j == nk - 1)
    def _():
        for jb in range(2):
            cols = slice(jb * LANES, (jb + 1) * LANES)
            acc = acc_sc[jb]
            num = jnp.where(low, acc[:tq, :LANES], acc[tq:, :LANES])
            den = jnp.where(low, acc[:tq, LANES:], acc[tq:, LANES:])
            o_ref[:, cols] = (num / den).astype(o_ref.dtype)


def _attn_a(q, k, v, B, T):
    n = q.shape[0]
    tq = A_TQ
    tk = min(A_TK, T)
    nq, nk = T // tq, T // tk
    return pl.pallas_call(
        functools.partial(_attn_a_kernel, nk=nk),
        grid=(B, nq, nk),
        in_specs=[
            pl.BlockSpec((tq, 256), lambda b, i, j: (b * nq + i, 0)),
            pl.BlockSpec((tk, LANES), lambda b, i, j: (b * nk + j, 0)),
            pl.BlockSpec((tk, LANES), lambda b, i, j: (b * nk + j, 0)),
        ],
        out_specs=pl.BlockSpec((tq, 256), lambda b, i, j: (b * nq + i, 0)),
        out_shape=jax.ShapeDtypeStruct((n, 256), BF16),
        scratch_shapes=[
            pltpu.VMEM((2, 2 * tq, 1), F32),
            pltpu.VMEM((2, 2 * tq, 256), F32),
        ],
        compiler_params=_cparams("parallel", "parallel", "arbitrary"),
        name="attn_a",
    )(q, k, v)


def _band_kernel(*refs, S, W, halo, align, L, kvw, mode, unroll):
    if mode == "sink":
        sink_ref, q_ref, k_ref, v_ref, tab_ref, o_ref = refs
    elif mode == "lse":
        q_ref, k_ref, v_ref, tab_ref, o_ref, lse_ref = refs
    else:
        q_ref, k_ref, v_ref, tab_ref, o_ref = refs
    qi = pl.program_id(1)
    G, TQ = q_ref.shape[0], q_ref.shape[1]
    per = TQ // S
    low = _low_lanes()
    ones = jnp.ones((W, LANES), BF16)
    first_half = lax.broadcasted_iota(jnp.int32, (2 * S, 1), 0) < S

    def body(it, carry):
        g = it // per
        i = it % per
        ig = qi * per + i
        start = pl.multiple_of(jnp.clip(ig * S - halo, 0, L - W), align)
        q0 = pl.multiple_of(i * S, S)
        qs = q_ref[g, pl.ds(q0, S), :]
        ks = k_ref[g, pl.ds(start, W), :]
        vs = v_ref[g, pl.ds(start, W), :]
        case = (ig * S - start) // (S if mode == "bias" else halo)
        for jb in range(2):
            cols = slice(jb * LANES, (jb + 1) * LANES)
            kcols = cols if kvw == 256 else slice(0, LANES)
            q2 = _stack_heads(qs[:, cols], low)
            s = lax.dot_general(q2, ks[:, kcols], NT_DIMS, preferred_element_type=F32)
            s = s + (tab_ref[case, jb] if mode == "bias" else tab_ref[case])
            m = jnp.max(s, axis=-1, keepdims=True)
            if mode == "sink":
                sk = jnp.where(first_half, sink_ref[jb, 0], sink_ref[jb, 1])
                m = jnp.maximum(m, sk)
            p = jnp.exp2(s - m).astype(BF16)
            v1 = jnp.concatenate([vs[:, kcols], ones], axis=1)
            pv = jnp.dot(p, v1, preferred_element_type=F32)
            l = pv[:, LANES:]
            if mode == "sink":
                l = l + jnp.exp2(sk - m)
            num = jnp.where(low, pv[:S, :LANES], pv[S:, :LANES])
            den = jnp.where(low, l[:S], l[S:])
            o_ref[g, pl.ds(q0, S), cols] = (num / den).astype(o_ref.dtype)
            if mode == "lse":
                lse_ref[g, pl.ds(q0, S), cols] = (jnp.where(low, m[:S], m[S:]) + jnp.log2(den)) * (1.0 / LOG2E)
        return carry

    lax.fori_loop(0, G * per, body, 0, unroll=unroll)


def _band_mask_table(S, W, R, halo):
    rowq = np.arange(2 * S) % S
    col = np.arange(W)
    tabs = [np.where(np.abs(col[None, :] - rowq[:, None] - c * halo) <= R, 0.0, NEG) for c in range(3)]
    return jnp.asarray(np.stack(tabs), F32)


def _band(q, k, v, *, nseq, L, S, W, R, halo, mode, extra=None):
    kvw = k.shape[1]
    W = min(W, L)
    TQ = min(L, BAND_TOKENS)
    G = max(1, BAND_TOKENS // L)
    S = min(S, L)
    align = math.gcd(S, halo, L - W) if L > W else S
    unroll = max(1, BAND_UNROLL_ROWS // S)
    assert nseq % G == 0 and L % TQ == 0 and TQ % S == 0 and align % 16 == 0
    q3 = q.reshape(nseq, L, 256)
    k3 = k.reshape(nseq, L, kvw)
    v3 = v.reshape(nseq, L, kvw)
    qspec = pl.BlockSpec((G, TQ, 256), lambda s, i: (s, i, 0))
    kvspec = pl.BlockSpec((G, L, kvw), lambda s, i: (s, 0, 0))
    in_specs = [qspec, kvspec, kvspec]
    args = [q3, k3, v3]
    out_specs = qspec
    out_shape = jax.ShapeDtypeStruct((nseq, L, 256), BF16)
    tab = extra if mode == "bias" else _band_mask_table(S, W, R, halo)
    in_specs = in_specs + [_resident_spec(tab.shape)]
    args = args + [tab]
    if mode == "sink":
        in_specs = [pl.BlockSpec(memory_space=pltpu.SMEM)] + in_specs
        args = [extra] + args
    elif mode == "lse":
        out_specs = [qspec, qspec]
        out_shape = [out_shape, jax.ShapeDtypeStruct((nseq, L, 256), F32)]
    out = pl.pallas_call(
        functools.partial(_band_kernel, S=S, W=W, halo=halo, align=align, L=L, kvw=kvw, mode=mode,
                          unroll=unroll),
        grid=(nseq // G, L // TQ),
        in_specs=in_specs,
        out_specs=out_specs,
        out_shape=out_shape,
        compiler_params=_cparams("parallel", "arbitrary"),
        name="band_" + mode,
    )(*args)
    if mode == "lse":
        return out[0].reshape(nseq * L, 256), out[1].reshape(nseq * L, 256)
    return out.reshape(nseq * L, 256)


def _mixout_kernel(x_ref, ya_ref, yb_ref, c1o, c1l, c2o, c2l, c3o, c3l, yd_ref,
                   gg_ref, wo_ref, gx_ref, wq_ref, xo_ref, qx_ref, *scr):
    tm = x_ref.shape[0]

    def gn(y, gi):
        return _rms_rows(y.astype(F32), gg_ref[gi:gi + 1, :]).astype(BF16)

    def token_order(ref, buf):
        d = ref.shape[1]
        for hb in range(2):
            for r in range(d):
                buf[hb, pl.ds(r, tm // d, stride=d), :] = ref[0, r, :, hb * LANES:(hb + 1) * LANES].astype(F32)
        return jnp.concatenate([buf[0], buf[1]], axis=-1)

    o1, l1 = c1o[...].astype(F32), c1l[...]
    o2, l2 = token_order(c2o, scr[0]), token_order(c2l, scr[1])
    o3, l3 = token_order(c3o, scr[2]), token_order(c3l, scr[3])
    mx = jnp.maximum(jnp.maximum(l1, l2), l3)
    e1, e2, e3 = jnp.exp(l1 - mx), jnp.exp(l2 - mx), jnp.exp(l3 - mx)
    yc = (e1 * o1 + e2 * o2 + e3 * o3) / (e1 + e2 + e3)
    ycat = jnp.concatenate([gn(ya_ref[...], 0), gn(yb_ref[...], 1), gn(yc, 2), gn(yd_ref[...], 3)], axis=-1)
    xn = x_ref[...] + jnp.dot(ycat, wo_ref[...], preferred_element_type=F32)
    xo_ref[...] = xn
    hq = _rms_rows(xn, gx_ref[...]).astype(BF16)
    qx_ref[...] = (jnp.dot(hq, wq_ref[...], preferred_element_type=F32) * (X_HEAD_DIM ** -0.5)).astype(BF16)


def _mixout(x, ya, yb, c1, c2, c3, yd, gg, wo, gx, wq, T):
    n = x.shape[0]
    B = n // T
    tm = TOKEN_TILE
    tpb = T // tm
    row = lambda wd: pl.BlockSpec((tm, wd), lambda i: (i, 0))
    slab = lambda d: pl.BlockSpec((1, d, tm // d, 256), lambda i: (i // tpb, 0, i % tpb, 0))
    d2, d3 = C_BRANCHES[1][1], C_BRANCHES[2][1]
    return pl.pallas_call(
        _mixout_kernel,
        grid=(n // tm,),
        in_specs=[row(D_MODEL)] + [row(256)] * 4 + [slab(d2)] * 2 + [slab(d3)] * 2 + [row(256)] + [
            _const_spec((4, GROUP_WIDTH)),
            _resident_spec((D_MODEL, D_MODEL)),
            _const_spec((1, D_MODEL)),
            _resident_spec((D_MODEL, D_MODEL)),
        ],
        out_specs=[row(D_MODEL), row(D_MODEL)],
        out_shape=[jax.ShapeDtypeStruct((n, D_MODEL), F32), jax.ShapeDtypeStruct((n, D_MODEL), BF16)],
        scratch_shapes=[pltpu.VMEM((2, tm, LANES), F32)] * 4,
        compiler_params=_cparams("parallel"),
        name="mixout",
    )(x, ya, yb, c1[0], c1[1], *[z.reshape(B, d2, T // d2, 256) for z in c2],
      *[z.reshape(B, d3, T // d3, 256) for z in c3], yd, gg, wo, gx, wq)


def _memkv_kernel(m_ref, g_ref, w_ref, kv_ref):
    h = _rms_rows(m_ref[...], g_ref[...]).astype(BF16)
    kv_ref[...] = jnp.dot(h, w_ref[...], preferred_element_type=F32).astype(BF16)


def _memkv(mem, g, w):
    n = mem.shape[0]
    return pl.pallas_call(
        _memkv_kernel,
        grid=(n // MEM_LEN,),
        in_specs=[
            pl.BlockSpec((MEM_LEN, D_MODEL), lambda i: (i, 0)),
            _const_spec((1, D_MODEL)),
            _resident_spec((D_MODEL, 2 * D_MODEL)),
        ],
        out_specs=pl.BlockSpec((MEM_LEN, 2 * D_MODEL), lambda i: (i, 0)),
        out_shape=jax.ShapeDtypeStruct((n, 2 * D_MODEL), BF16),
        compiler_params=_cparams("parallel"),
        name="memkv",
    )(mem, g, w)


def _cross_kernel(qx_ref, kv_ref, x_ref, wo_ref, xo_ref):
    outs = []
    for h in range(X_HEADS):
        cols = slice(h * X_HEAD_DIM, (h + 1) * X_HEAD_DIM)
        vcols = slice(D_MODEL + h * X_HEAD_DIM, D_MODEL + (h + 1) * X_HEAD_DIM)
        s = lax.dot_general(qx_ref[:, cols], kv_ref[:, cols], NT_DIMS, preferred_element_type=F32)
        m = jnp.max(s, axis=-1, keepdims=True)
        p = jnp.exp(s - m)
        l = jnp.sum(p, axis=-1, keepdims=True)
        o = jnp.dot(p.astype(BF16), kv_ref[:, vcols], preferred_element_type=F32) / l
        outs.append(o.astype(BF16))
    o = jnp.concatenate(outs, axis=-1)
    xo_ref[...] = x_ref[...] + jnp.dot(o, wo_ref[...], preferred_element_type=F32)


def _cross(qx, kv, x, wo, T):
    n = x.shape[0]
    tm = TOKEN_TILE
    tpb = T // tm
    row = lambda wd: pl.BlockSpec((tm, wd), lambda i: (i, 0))
    return pl.pallas_call(
        _cross_kernel,
        grid=(n // tm,),
        in_specs=[
            row(D_MODEL),
            pl.BlockSpec((MEM_LEN, 2 * D_MODEL), lambda i: (i // tpb, 0)),
            row(D_MODEL),
            _resident_spec((D_MODEL, D_MODEL)),
        ],
        out_specs=row(D_MODEL),
        out_shape=jax.ShapeDtypeStruct((n, D_MODEL), F32),
        compiler_params=_cparams("parallel"),
        name="cross",
    )(qx, kv, x, wo)


HALO = 16


def _ffn_kernel(xp_ref, x_ref, xn_ref, g_ref, wu_ref, cw_ref, cb_ref, wd_ref, gf_ref, o_ref, *, tpb, final):
    i = pl.program_id(0)
    tm = x_ref.shape[0]
    x = x_ref[...]
    xe = jnp.concatenate([xp_ref[...], x, xn_ref[...]], axis=0)
    he = _rms_rows(xe, g_ref[...]).astype(BF16)
    hc = he[HALO:HALO + tm]
    rows = lax.broadcasted_iota(jnp.int32, (tm + 2 * HALO, 1), 0)
    seq_first = (i % tpb) == 0
    seq_last = (i % tpb) == tpb - 1
    keep = jnp.logical_and(jnp.logical_or(rows >= HALO, jnp.logical_not(seq_first)),
                           jnp.logical_or(rows < HALO + tm, jnp.logical_not(seq_last)))
    keep = keep.astype(F32)
    acc = x
    for c0, c1 in FF_CHUNKS:
        gc = slice(c0, c1)
        uc = slice(D_FF + c0, D_FF + c1)
        ge = jnp.dot(he, wu_ref[:, gc], preferred_element_type=F32) * keep
        val = jnp.dot(hc, wu_ref[:, uc], preferred_element_type=F32)
        gate = (ge[HALO - 1:HALO - 1 + tm] * cw_ref[0:1, gc] + ge[HALO:HALO + tm] * cw_ref[1:2, gc]
                + ge[HALO + 1:HALO + 1 + tm] * cw_ref[2:3, gc] + cb_ref[:, gc])
        act = 0.5 * gate * (1.0 + lax.erf(gate * (2.0 ** -0.5))) * val
        acc = acc + jnp.dot(act.astype(BF16), wd_ref[gc, :], preferred_element_type=F32)
    if final:
        acc = _rms_rows(acc, gf_ref[...])
    o_ref[...] = acc


def _ffn(x, g, wu, cw, cb, wd, gf, T, final):
    n = x.shape[0]
    tm = FFN_TILE
    tpb = T // tm
    hb = tm // HALO
    nhb = n // HALO
    return pl.pallas_call(
        functools.partial(_ffn_kernel, tpb=tpb, final=final),
        grid=(n // tm,),
        in_specs=[
            pl.BlockSpec((HALO, D_MODEL), lambda i: (jnp.maximum(i * hb - 1, 0), 0)),
            pl.BlockSpec((tm, D_MODEL), lambda i: (i, 0)),
            pl.BlockSpec((HALO, D_MODEL), lambda i: (jnp.minimum((i + 1) * hb, nhb - 1), 0)),
            _const_spec((1, D_MODEL)),
            _resident_spec((D_MODEL, 2 * D_FF)),
            _const_spec((3, D_FF)),
            _const_spec((1, D_FF)),
            _resident_spec((D_FF, D_MODEL)),
            _const_spec((1, D_MODEL)),
        ],
        out_specs=pl.BlockSpec((tm, D_MODEL), lambda i: (i, 0)),
        out_shape=jax.ShapeDtypeStruct((n, D_MODEL), F32),
        compiler_params=_cparams("parallel"),
        name="ffn",
    )(x, x, x, g, wu, cw, cb, wd, gf)


def _rope_tables(T):
    t = jnp.arange(T)

    def parts(pos, half):
        freqs = ROPE_THETA ** (-jnp.arange(half, dtype=F32) / half)
        ang = pos.astype(F32)[:, None] * freqs[None, :]
        c, s, z = jnp.cos(ang), jnp.sin(ang), jnp.zeros_like(ang)
        return jnp.concatenate([c, c], -1), jnp.concatenate([-s, z], -1), jnp.concatenate([z, s], -1)

    lin = parts(t, HEAD_DIM // 2)
    row = parts(t // GRID_W, HEAD_DIM // 4)
    colp = parts(t % GRID_W, HEAD_DIM // 4)
    ax = [jnp.concatenate([a, b], -1) for a, b in zip(row, colp)]
    return jnp.concatenate([jnp.tile(z, (1, 2)) for z in (*lin, *ax)], axis=-1)


def _na_bias_table(rpb):
    c = np.arange(GRID_W)
    qstart = np.clip(c - NA_COLS // 2, 0, GRID_W - NA_COLS)
    rel = c[None, :] - qstart[:, None]
    valid = (rel >= 0) & (rel < NA_COLS)
    dc = np.clip(c[None, :] - c[:, None], -(NA_COLS - 1), NA_COLS - 1) + (NA_COLS - 1)
    e = np.arange(NA_ROWS)
    m = np.arange(NA_ROWS)
    dr = m[None, :] - e[:, None] + (NA_ROWS - 1)
    onehot = np.zeros((2 * NA_COLS - 1, GRID_W * GRID_W), np.float32)
    onehot[dc.reshape(-1), np.arange(GRID_W * GRID_W)] = 1.0
    rows = rpb[:, dr].reshape(-1, 2 * NA_COLS - 1)
    tbl = jnp.dot(rows, onehot, precision=lax.Precision.HIGHEST)
    tbl = tbl.reshape(rpb.shape[0], NA_ROWS, NA_ROWS, GRID_W, GRID_W)
    tbl = jnp.where(valid[None, None, None], tbl, NEG)
    tbl = tbl.transpose(1, 0, 3, 2, 4)
    return tbl.reshape(NA_ROWS, 2, 2 * GRID_W, NA_ROWS * GRID_W).astype(F32)


def _gqa_heads(z, axis):
    parts = [lax.slice_in_dim(z, hd * HEAD_DIM, (hd + 1) * HEAD_DIM, axis=axis) for hd in GQA_HEAD_ORDER]
    return jnp.concatenate(parts, axis=axis)


def _trunk(x, mem, p):
    B, T, _ = x.shape
    n = B * T
    x = x.reshape(n, D_MODEL)
    tab = _rope_tables(T)
    bd = jnp.asarray(np.kron(np.eye(2), np.ones((HEAD_DIM, HEAD_DIM))), BF16)
    mem2 = mem.reshape(B * MEM_LEN, D_MODEL)
    for l in range(DEPTH):
        qa, ka, va, qb, kb, vb, qc, kc, vc, qc4, kc4, vc4, qc16, kc16, vc16, qd, kd, vd = _proj(
            x, p["norm_mix_g"][l], p["w_in"][l], tab, p["gq"][l], p["gk"][l], bd, T)
        ya = _attn_a(qa, ka, va, B, T)
        yb = _band(qb, kb, vb, nseq=B, L=T, S=BAND_BLOCK, W=BAND_BLOCK + 2 * B_RADIUS, R=B_RADIUS,
                   halo=B_RADIUS, mode="sink", extra=p["sink"][l])
        cs = []
        for (window, d), qkv in zip(C_BRANCHES, ((qc, kc, vc), (qc4, kc4, vc4), (qc16, kc16, vc16))):
            r = window // (2 * d)
            blk = min(C_BLOCK, T // d)
            qs, ks, vs = (z.reshape(n, 256) for z in qkv)
            cs.append(_band(qs, ks, vs, nseq=B * d, L=T // d, S=blk, W=blk + 2 * r, R=r, halo=r, mode="lse"))
        yd = _band(qd, kd, vd, nseq=B, L=T, S=GRID_W, W=NA_ROWS * GRID_W, R=0, halo=(NA_ROWS // 2) * GRID_W,
                   mode="bias", extra=p["na_bias"][l])
        x, qx = _mixout(x, ya, yb, cs[0], cs[1], cs[2], yd, p["grp_g"][l], p["w_out"][l],
                        p["norm_x_g"][l], p["w_xq"][l], T)
        kv = _memkv(mem2, p["norm_mem_g"][l], p["w_xkv"][l])
        x = _cross(qx, kv, x, p["w_xo"][l], T)
        x = _ffn(x, p["norm_ffn_g"][l], p["w_up"][l], p["conv_w"][l], p["conv_b"][l], p["w_down"][l],
                 p["final_g"], T, final=(l == DEPTH - 1))
    return x.reshape(B, T, D_MODEL)


def _prepare(norm_mix_g, w_in, a_q_norm_g, a_k_norm_g, b_sink, d_rpb, grp_norm_g, w_out, norm_x_g,
             norm_mem_g, w_xq, w_xkv, w_xo, norm_ffn_g, w_up, conv_w, conv_b, w_down, final_norm_g):
    layers = range(norm_mix_g.shape[0])

    def w_in_layer(w):
        return jnp.concatenate([_gqa_heads(w[:, 0:256], 1), w[:, 256:512], _gqa_heads(w[:, 512:768], 1),
                                w[:, 768:]], axis=1).astype(BF16)

    def w_out_layer(w):
        return jnp.concatenate([_gqa_heads(w[0:256], 0), _gqa_heads(w[256:512], 0), w[512:]], axis=0).astype(BF16)

    def grp_layer(g):
        return jnp.concatenate([_gqa_heads(g[0:2], 1), g[2:4]], axis=0)

    return {
        "norm_mix_g": [norm_mix_g[l][None, :] for l in layers],
        "w_in": [w_in_layer(w_in[l]) for l in layers],
        "gq": [jnp.tile(a_q_norm_g[l], 2)[None, :] * (HEAD_DIM ** -0.5 * LOG2E) for l in layers],
        "gk": [jnp.tile(a_k_norm_g[l], 2)[None, :] for l in layers],
        "sink": [b_sink[l][np.array(GQA_HEAD_ORDER)].reshape(2, 2) * LOG2E for l in layers],
        "na_bias": [_na_bias_table(d_rpb[l] * LOG2E) for l in layers],
        "grp_g": [grp_layer(grp_norm_g[l]) for l in layers],
        "w_out": [w_out_layer(w_out[l]) for l in layers],
        "norm_x_g": [norm_x_g[l][None, :] for l in layers],
        "norm_mem_g": [norm_mem_g[l][None, :] for l in layers],
        "w_xq": [w_xq[l].astype(BF16) for l in layers],
        "w_xkv": [w_xkv[l].astype(BF16) for l in layers],
        "w_xo": [w_xo[l].astype(BF16) for l in layers],
        "norm_ffn_g": [norm_ffn_g[l][None, :] for l in layers],
        "w_up": [w_up[l].astype(BF16) for l in layers],
        "conv_w": [conv_w[l] for l in layers],
        "conv_b": [conv_b[l][None, :] for l in layers],
        "w_down": [w_down[l].astype(BF16) for l in layers],
        "final_g": final_norm_g[None, :],
    }


def kernel(x_prompt, x_sample, mem_prompt, mem_sample, norm_mix_g, w_in, a_q_norm_g, a_k_norm_g, b_sink,
           d_rpb, grp_norm_g, w_out, norm_x_g, norm_mem_g, w_xq, w_xkv, w_xo, norm_ffn_g, w_up, conv_w,
           conv_b, w_down, final_norm_g):
    p = _prepare(norm_mix_g, w_in, a_q_norm_g, a_k_norm_g, b_sink, d_rpb, grp_norm_g, w_out, norm_x_g,
                 norm_mem_g, w_xq, w_xkv, w_xo, norm_ffn_g, w_up, conv_w, conv_b, w_down, final_norm_g)
    return _trunk(x_prompt, mem_prompt, p), _trunk(x_sample, mem_sample, p)
```

```python
import functools
import math

import numpy as np
import jax
import jax.numpy as jnp
from jax import lax
from jax.experimental import pallas as pl
from jax.experimental.pallas import tpu as pltpu

F32 = jnp.float32
BF16 = jnp.bfloat16

D_MODEL = 1024
DEPTH = 4
HEAD_DIM = 64
GROUP_WIDTH = 256
GRID_W = 64
ROPE_THETA = 10000.0
B_RADIUS = 128
C_BRANCHES = ((128, 1), (512, 4), (2048, 16))
NA_ROWS = 8
NA_COLS = 16
MEM_LEN = 256
X_HEADS = 4
X_HEAD_DIM = D_MODEL // X_HEADS
D_FF = 2816
EPS = 1e-6
NEG = -1e30
IN_WIDTH = 2560

LANES = 128
VMEM_LIMIT = 56 * 1024 * 1024
TOKEN_TILE = 1024
FFN_TILE = 512
BAND_TOKENS = 2048
A_TQ = 512
A_TQ_ONE_BLOCK = 1024
A_TK = 8192
A_ROWS = 128
BAND_BLOCK = 256
C_BLOCK = 128
BAND_UNROLL_ROWS = 1024
LOG2E = 1.4426950408889634
MXU_COLS = 256
FF_CHUNKS = ((0, 6 * MXU_COLS), (6 * MXU_COLS, D_FF))

GQA_HEAD_ORDER = (0, 2, 1, 3)
NT_DIMS = (((1,), (1,)), ((), ()))


def _cparams(*sem):
    return pltpu.CompilerParams(dimension_semantics=sem, vmem_limit_bytes=VMEM_LIMIT)


def _const_spec(shape):
    n = len(shape)
    return pl.BlockSpec(shape, lambda *_: (0,) * n)


def _resident_spec(shape):
    n = len(shape)
    return pl.BlockSpec(shape, lambda *_: (0,) * n, pipeline_mode=pl.Buffered(1))


def _rms_rows(x, g):
    return x * lax.rsqrt(jnp.mean(x * x, axis=-1, keepdims=True) + EPS) * g


def _low_lanes():
    return lax.broadcasted_iota(jnp.int32, (1, LANES), 1) < HEAD_DIM


def _stack_heads(qb, low):
    zero = jnp.zeros_like(qb)
    return jnp.concatenate([jnp.where(low, qb, zero), jnp.where(low, zero, qb)], axis=0)


def _rope(y, c, sa, sb, shift):
    up = pltpu.roll(y, LANES - shift, 1)
    dn = pltpu.roll(y, shift, 1)
    return y * c + up * sa + dn * sb


def _proj_kernel(x_ref, g_ref, w_ref, tab_ref, gq_ref, gk_ref, bd_ref,
                 qa, ka, va, qb, kb, vb, qc, kc, vc, qc4, kc4, vc4, qc16, kc16, vc16, qd, kd, vd,
                 q_scr, k_scr, v_scr):
    tm = x_ref.shape[0]
    scale = HEAD_DIM ** -0.5 * LOG2E
    h = _rms_rows(x_ref[...], g_ref[...]).astype(BF16)

    def halves(y):
        return y[:, :LANES], y[:, LANES:]

    def tab(k):
        return tab_ref[:, k * LANES:(k + 1) * LANES]

    def rope_lin(y):
        return _rope(y, tab(0), tab(1), tab(2), HEAD_DIM // 2)

    def rope_ax(y):
        return _rope(y, tab(3), tab(4), tab(5), HEAD_DIM // 4)

    def qknorm(y, g):
        ss = jnp.dot((y * y).astype(BF16), bd_ref[...], preferred_element_type=F32) * (1.0 / HEAD_DIM)
        return y * lax.rsqrt(ss + EPS) * g

    def emit_c(scr, y0, y1, nat, subs):
        for hb, y in enumerate((y0, y1)):
            cols = slice(hb * LANES, (hb + 1) * LANES)
            scr[hb] = y
            nat[:, cols] = y.astype(BF16)
            for (_, d), ref in zip(C_BRANCHES[1:], subs):
                for r in range(d):
                    ref[0, r, :, cols] = scr[hb, pl.ds(r, tm // d, stride=d), :].astype(BF16)

    ys = [halves(jnp.dot(h, w_ref[:, c0:c0 + 256], preferred_element_type=F32))
          for c0 in range(0, IN_WIDTH, 256)]

    y0, y1 = ys[0]
    qa[:, :LANES] = rope_ax(qknorm(y0, gq_ref[...])).astype(BF16)
    qa[:, LANES:] = rope_ax(qknorm(y1, gq_ref[...])).astype(BF16)
    y0, y1 = ys[1]
    ka[...] = rope_ax(qknorm(y0, gk_ref[...])).astype(BF16)
    va[...] = y1.astype(BF16)

    y0, y1 = ys[2]
    qb[:, :LANES] = rope_lin(y0 * scale).astype(BF16)
    qb[:, LANES:] = rope_lin(y1 * scale).astype(BF16)
    y0, y1 = ys[3]
    kb[...] = rope_lin(y0).astype(BF16)
    vb[...] = y1.astype(BF16)

    y0, y1 = ys[4]
    emit_c(q_scr, rope_lin(y0 * scale), rope_lin(y1 * scale), qc, (qc4, qc16))
    y0, y1 = ys[5]
    emit_c(k_scr, rope_lin(y0), rope_lin(y1), kc, (kc4, kc16))
    y0, y1 = ys[6]
    emit_c(v_scr, y0, y1, vc, (vc4, vc16))

    y0, y1 = ys[7]
    qd[:, :LANES] = (y0 * scale).astype(BF16)
    qd[:, LANES:] = (y1 * scale).astype(BF16)
    for (y0, y1), ref in zip(ys[8:], (kd, vd)):
        ref[:, :LANES] = y0.astype(BF16)
        ref[:, LANES:] = y1.astype(BF16)


def _proj(x, g, w, tab, gq, gk, bd, T):
    n = x.shape[0]
    tm = TOKEN_TILE
    tpb = T // tm
    B = n // T
    row = lambda wd: (pl.BlockSpec((tm, wd), lambda i: (i, 0)), jax.ShapeDtypeStruct((n, wd), BF16))

    def slab(d):
        return (pl.BlockSpec((1, d, tm // d, 256), lambda i: (i // tpb, 0, i % tpb, 0)),
                jax.ShapeDtypeStruct((B, d, T // d, 256), BF16))

    outs = ([row(256), row(128), row(128)] * 2 + [row(256)] * 3
            + [slab(d) for _, d in C_BRANCHES[1:] for _ in range(3)] + [row(256)] * 3)
    return pl.pallas_call(
        _proj_kernel,
        grid=(n // tm,),
        in_specs=[
            pl.BlockSpec((tm, D_MODEL), lambda i: (i, 0)),
            _const_spec((1, D_MODEL)),
            _resident_spec((D_MODEL, IN_WIDTH)),
            pl.BlockSpec((tm, 6 * LANES), lambda i: (i % tpb, 0)),
            _const_spec((1, LANES)),
            _const_spec((1, LANES)),
            _const_spec((LANES, LANES)),
        ],
        out_specs=[o[0] for o in outs],
        out_shape=[o[1] for o in outs],
        scratch_shapes=[pltpu.VMEM((2, tm, LANES), F32)] * 3,
        compiler_params=_cparams("parallel"),
        name="proj",
    )(x, g, w, tab, gq, gk, bd)


def _attn_a_kernel(q_ref, k_ref, v_ref, o_ref, m_sc, acc_sc, *, nk):
    j = pl.program_id(2)
    tq = q_ref.shape[0]
    tk = k_ref.shape[0]

    @pl.when(j == 0)
    def _():
        m_sc[...] = jnp.full(m_sc.shape, NEG, F32)
        acc_sc[...] = jnp.zeros(acc_sc.shape, F32)

    low = _low_lanes()
    k = k_ref[...]
    v1 = jnp.concatenate([v_ref[...], jnp.ones((tk, LANES), BF16)], axis=1)
    for jb in range(2):
        q2 = _stack_heads(q_ref[:, jb * LANES:(jb + 1) * LANES], low)
        for rb in range(2 * tq // A_ROWS):
            rows = slice(rb * A_ROWS, (rb + 1) * A_ROWS)
            s = lax.dot_general(q2[rows], k, NT_DIMS, preferred_element_type=F32)
            m = m_sc[jb, rows, :]
            m_new = jnp.maximum(m, jnp.max(s, axis=-1, keepdims=True))
            p = jnp.exp2(s - m_new).astype(BF16)
            acc_sc[jb, rows, :] = (acc_sc[jb, rows, :] * jnp.exp2(m - m_new)
                                   + jnp.dot(p, v1, preferred_element_type=F32))
            m_sc[jb, rows, :] = m_new

    @pl.when(j == nk - 1)
    def _():
        for jb in range(2):
            cols = slice(jb * LANES, (jb + 1) * LANES)
            acc = acc_sc[jb]
            num = jnp.where(low, acc[:tq, :LANES], acc[tq:, :LANES])
            den = jnp.where(low, acc[:tq, LANES:], acc[tq:, LANES:])
            o_ref[:, cols] = (num / den).astype(o_ref.dtype)


def _attn_a(q, k, v, B, T):
    n = q.shape[0]
    tk = min(A_TK, T)
    tq = A_TQ_ONE_BLOCK if tk == T else A_TQ
    nq, nk = T // tq, T // tk
    return pl.pallas_call(
        functools.partial(_attn_a_kernel, nk=nk),
        grid=(B, nq, nk),
        in_specs=[
            pl.BlockSpec((tq, 256), lambda b, i, j: (b * nq + i, 0)),
            pl.BlockSpec((tk, LANES), lambda b, i, j: (b * nk + j, 0)),
            pl.BlockSpec((tk, LANES), lambda b, i, j: (b * nk + j, 0)),
        ],
        out_specs=pl.BlockSpec((tq, 256), lambda b, i, j: (b * nq + i, 0)),
        out_shape=jax.ShapeDtypeStruct((n, 256), BF16),
        scratch_shapes=[
            pltpu.VMEM((2, 2 * tq, 1), F32),
            pltpu.VMEM((2, 2 * tq, 256), F32),
        ],
        compiler_params=_cparams("parallel", "parallel", "arbitrary"),
        name="attn_a",
    )(q, k, v)


def _band_kernel(*refs, S, W, halo, align, L, kvw, mode, unroll):
    if mode == "sink":
        sink_ref, q_ref, k_ref, v_ref, tab_ref, o_ref = refs
    elif mode == "lse":
        q_ref, k_ref, v_ref, tab_ref, o_ref, lse_ref = refs
    else:
        q_ref, k_ref, v_ref, tab_ref, o_ref = refs
    qi = pl.program_id(1)
    G, TQ = q_ref.shape[0], q_ref.shape[1]
    per = TQ // S
    low = _low_lanes()
    ones = jnp.ones((W, LANES), BF16)
    first_half = lax.broadcasted_iota(jnp.int32, (2 * S, 1), 0) < S

    def body(it, carry):
        g = it // per
        i = it % per
        ig = qi * per + i
        start = pl.multiple_of(jnp.clip(ig * S - halo, 0, L - W), align)
        q0 = pl.multiple_of(i * S, S)
        qs = q_ref[g, pl.ds(q0, S), :]
        ks = k_ref[g, pl.ds(start, W), :]
        vs = v_ref[g, pl.ds(start, W), :]
        case = (ig * S - start) // (S if mode == "bias" else halo)
        for jb in range(2):
            cols = slice(jb * LANES, (jb + 1) * LANES)
            kcols = cols if kvw == 256 else slice(0, LANES)
            q2 = _stack_heads(qs[:, cols], low)
            s = lax.dot_general(q2, ks[:, kcols], NT_DIMS, preferred_element_type=F32)
            s = s + (tab_ref[case, jb] if mode == "bias" else tab_ref[case])
            m = jnp.max(s, axis=-1, keepdims=True)
            if mode == "sink":
                sk = jnp.where(first_half, sink_ref[jb, 0], sink_ref[jb, 1])
                m = jnp.maximum(m, sk)
            p = jnp.exp2(s - m).astype(BF16)
            v1 = jnp.concatenate([vs[:, kcols], ones], axis=1)
            pv = jnp.dot(p, v1, preferred_element_type=F32)
            l = pv[:, LANES:]
            if mode == "sink":
                l = l + jnp.exp2(sk - m)
            num = jnp.where(low, pv[:S, :LANES], pv[S:, :LANES])
            den = jnp.where(low, l[:S], l[S:])
            o_ref[g, pl.ds(q0, S), cols] = (num / den).astype(o_ref.dtype)
            if mode == "lse":
                lse_ref[g, pl.ds(q0, S), cols] = (jnp.where(low, m[:S], m[S:]) + jnp.log2(den)) * (1.0 / LOG2E)
        return carry

    lax.fori_loop(0, G * per, body, 0, unroll=unroll)


def _band_mask_table(S, W, R, halo):
    rowq = np.arange(2 * S) % S
    col = np.arange(W)
    tabs = [np.where(np.abs(col[None, :] - rowq[:, None] - c * halo) <= R, 0.0, NEG) for c in range(3)]
    return jnp.asarray(np.stack(tabs), F32)


def _band(q, k, v, *, nseq, L, S, W, R, halo, mode, extra=None):
    kvw = k.shape[1]
    W = min(W, L)
    TQ = min(L, BAND_TOKENS)
    G = max(1, BAND_TOKENS // L)
    S = min(S, L)
    align = math.gcd(S, halo, L - W) if L > W else S
    unroll = max(1, BAND_UNROLL_ROWS // S)
    assert nseq % G == 0 and L % TQ == 0 and TQ % S == 0 and align % 16 == 0
    q3 = q.reshape(nseq, L, 256)
    k3 = k.reshape(nseq, L, kvw)
    v3 = v.reshape(nseq, L, kvw)
    qspec = pl.BlockSpec((G, TQ, 256), lambda s, i: (s, i, 0))
    kvspec = pl.BlockSpec((G, L, kvw), lambda s, i: (s, 0, 0))
    in_specs = [qspec, kvspec, kvspec]
    args = [q3, k3, v3]
    out_specs = qspec
    out_shape = jax.ShapeDtypeStruct((nseq, L, 256), BF16)
    tab = extra if mode == "bias" else _band_mask_table(S, W, R, halo)
    in_specs = in_specs + [_resident_spec(tab.shape)]
    args = args + [tab]
    if mode == "sink":
        in_specs = [pl.BlockSpec(memory_space=pltpu.SMEM)] + in_specs
        args = [extra] + args
    elif mode == "lse":
        out_specs = [qspec, qspec]
        out_shape = [out_shape, jax.ShapeDtypeStruct((nseq, L, 256), F32)]
    out = pl.pallas_call(
        functools.partial(_band_kernel, S=S, W=W, halo=halo, align=align, L=L, kvw=kvw, mode=mode,
                          unroll=unroll),
        grid=(nseq // G, L // TQ),
        in_specs=in_specs,
        out_specs=out_specs,
        out_shape=out_shape,
        compiler_params=_cparams("parallel", "arbitrary"),
        name="band_" + mode,
    )(*args)
    if mode == "lse":
        return out[0].reshape(nseq * L, 256), out[1].reshape(nseq * L, 256)
    return out.reshape(nseq * L, 256)


def _mixout_kernel(x_ref, ya_ref, yb_ref, c1o, c1l, c2o, c2l, c3o, c3l, yd_ref,
                   gg_ref, wo_ref, gx_ref, wq_ref, xo_ref, qx_ref, *scr):
    tm = x_ref.shape[0]

    def gn(y, gi):
        return _rms_rows(y.astype(F32), gg_ref[gi:gi + 1, :]).astype(BF16)

    def token_order(ref, buf):
        d = ref.shape[1]
        for hb in range(2):
            for r in range(d):
                buf[hb, pl.ds(r, tm // d, stride=d), :] = ref[0, r, :, hb * LANES:(hb + 1) * LANES].astype(F32)
        return jnp.concatenate([buf[0], buf[1]], axis=-1)

    o1, l1 = c1o[...].astype(F32), c1l[...]
    o2, l2 = token_order(c2o, scr[0]), token_order(c2l, scr[1])
    o3, l3 = token_order(c3o, scr[2]), token_order(c3l, scr[3])
    mx = jnp.maximum(jnp.maximum(l1, l2), l3)
    e1, e2, e3 = jnp.exp(l1 - mx), jnp.exp(l2 - mx), jnp.exp(l3 - mx)
    yc = (e1 * o1 + e2 * o2 + e3 * o3) / (e1 + e2 + e3)
    ycat = jnp.concatenate([gn(ya_ref[...], 0), gn(yb_ref[...], 1), gn(yc, 2), gn(yd_ref[...], 3)], axis=-1)
    xn = x_ref[...] + jnp.dot(ycat, wo_ref[...], preferred_element_type=F32)
    xo_ref[...] = xn
    hq = _rms_rows(xn, gx_ref[...]).astype(BF16)
    qx_ref[...] = (jnp.dot(hq, wq_ref[...], preferred_element_type=F32) * (X_HEAD_DIM ** -0.5)).astype(BF16)


def _mixout(x, ya, yb, c1, c2, c3, yd, gg, wo, gx, wq, T):
    n = x.shape[0]
    B = n // T
    tm = TOKEN_TILE
    tpb = T // tm
    row = lambda wd: pl.BlockSpec((tm, wd), lambda i: (i, 0))
    slab = lambda d: pl.BlockSpec((1, d, tm // d, 256), lambda i: (i // tpb, 0, i % tpb, 0))
    d2, d3 = C_BRANCHES[1][1], C_BRANCHES[2][1]
    return pl.pallas_call(
        _mixout_kernel,
        grid=(n // tm,),
        in_specs=[row(D_MODEL)] + [row(256)] * 4 + [slab(d2)] * 2 + [slab(d3)] * 2 + [row(256)] + [
            _const_spec((4, GROUP_WIDTH)),
            _resident_spec((D_MODEL, D_MODEL)),
            _const_spec((1, D_MODEL)),
            _resident_spec((D_MODEL, D_MODEL)),
        ],
        out_specs=[row(D_MODEL), row(D_MODEL)],
        out_shape=[jax.ShapeDtypeStruct((n, D_MODEL), F32), jax.ShapeDtypeStruct((n, D_MODEL), BF16)],
        scratch_shapes=[pltpu.VMEM((2, tm, LANES), F32)] * 4,
        compiler_params=_cparams("parallel"),
        name="mixout",
    )(x, ya, yb, c1[0], c1[1], *[z.reshape(B, d2, T // d2, 256) for z in c2],
      *[z.reshape(B, d3, T // d3, 256) for z in c3], yd, gg, wo, gx, wq)


def _memkv_kernel(m_ref, g_ref, w_ref, kv_ref):
    h = _rms_rows(m_ref[...], g_ref[...]).astype(BF16)
    kv_ref[...] = jnp.dot(h, w_ref[...], preferred_element_type=F32).astype(BF16)


def _memkv(mem, g, w):
    n = mem.shape[0]
    return pl.pallas_call(
        _memkv_kernel,
        grid=(n // MEM_LEN,),
        in_specs=[
            pl.BlockSpec((MEM_LEN, D_MODEL), lambda i: (i, 0)),
            _const_spec((1, D_MODEL)),
            _resident_spec((D_MODEL, 2 * D_MODEL)),
        ],
        out_specs=pl.BlockSpec((MEM_LEN, 2 * D_MODEL), lambda i: (i, 0)),
        out_shape=jax.ShapeDtypeStruct((n, 2 * D_MODEL), BF16),
        compiler_params=_cparams("parallel"),
        name="memkv",
    )(mem, g, w)


def _cross_kernel(qx_ref, kv_ref, x_ref, wo_ref, xo_ref):
    outs = []
    for h in range(X_HEADS):
        cols = slice(h * X_HEAD_DIM, (h + 1) * X_HEAD_DIM)
        vcols = slice(D_MODEL + h * X_HEAD_DIM, D_MODEL + (h + 1) * X_HEAD_DIM)
        s = lax.dot_general(qx_ref[:, cols], kv_ref[:, cols], NT_DIMS, preferred_element_type=F32)
        m = jnp.max(s, axis=-1, keepdims=True)
        p = jnp.exp(s - m)
        l = jnp.sum(p, axis=-1, keepdims=True)
        o = jnp.dot(p.astype(BF16), kv_ref[:, vcols], preferred_element_type=F32) / l
        outs.append(o.astype(BF16))
    o = jnp.concatenate(outs, axis=-1)
    xo_ref[...] = x_ref[...] + jnp.dot(o, wo_ref[...], preferred_element_type=F32)


def _cross(qx, kv, x, wo, T):
    n = x.shape[0]
    tm = TOKEN_TILE
    tpb = T // tm
    row = lambda wd: pl.BlockSpec((tm, wd), lambda i: (i, 0))
    return pl.pallas_call(
        _cross_kernel,
        grid=(n // tm,),
        in_specs=[
            row(D_MODEL),
            pl.BlockSpec((MEM_LEN, 2 * D_MODEL), lambda i: (i // tpb, 0)),
            row(D_MODEL),
            _resident_spec((D_MODEL, D_MODEL)),
        ],
        out_specs=row(D_MODEL),
        out_shape=jax.ShapeDtypeStruct((n, D_MODEL), F32),
        compiler_params=_cparams("parallel"),
        name="cross",
    )(qx, kv, x, wo)


HALO = 16


def _ffn_kernel(xp_ref, x_ref, xn_ref, g_ref, wu_ref, cw_ref, cb_ref, wd_ref, gf_ref, o_ref, *, tpb, final):
    i = pl.program_id(0)
    tm = x_ref.shape[0]
    x = x_ref[...]
    xe = jnp.concatenate([xp_ref[...], x, xn_ref[...]], axis=0)
    he = _rms_rows(xe, g_ref[...]).astype(BF16)
    hc = he[HALO:HALO + tm]
    rows = lax.broadcasted_iota(jnp.int32, (tm + 2 * HALO, 1), 0)
    seq_first = (i % tpb) == 0
    seq_last = (i % tpb) == tpb - 1
    keep = jnp.logical_and(jnp.logical_or(rows >= HALO, jnp.logical_not(seq_first)),
                           jnp.logical_or(rows < HALO + tm, jnp.logical_not(seq_last)))
    keep = keep.astype(F32)
    acc = x
    for c0, c1 in FF_CHUNKS:
        gc = slice(c0, c1)
        uc = slice(D_FF + c0, D_FF + c1)
        ge = jnp.dot(he, wu_ref[:, gc], preferred_element_type=F32) * keep
        val = jnp.dot(hc, wu_ref[:, uc], preferred_element_type=F32)
        gate = (ge[HALO - 1:HALO - 1 + tm] * cw_ref[0:1, gc] + ge[HALO:HALO + tm] * cw_ref[1:2, gc]
                + ge[HALO + 1:HALO + 1 + tm] * cw_ref[2:3, gc] + cb_ref[:, gc])
        act = 0.5 * gate * (1.0 + lax.erf(gate * (2.0 ** -0.5))) * val
        acc = acc + jnp.dot(act.astype(BF16), wd_ref[gc, :], preferred_element_type=F32)
    if final:
        acc = _rms_rows(acc, gf_ref[...])
    o_ref[...] = acc


def _ffn(x, g, wu, cw, cb, wd, gf, T, final):
    n = x.shape[0]
    tm = FFN_TILE
    tpb = T // tm
    hb = tm // HALO
    nhb = n // HALO
    return pl.pallas_call(
        functools.partial(_ffn_kernel, tpb=tpb, final=final),
        grid=(n // tm,),
        in_specs=[
            pl.BlockSpec((HALO, D_MODEL), lambda i: (jnp.maximum(i * hb - 1, 0), 0)),
            pl.BlockSpec((tm, D_MODEL), lambda i: (i, 0)),
            pl.BlockSpec((HALO, D_MODEL), lambda i: (jnp.minimum((i + 1) * hb, nhb - 1), 0)),
            _const_spec((1, D_MODEL)),
            _resident_spec((D_MODEL, 2 * D_FF)),
            _const_spec((3, D_FF)),
            _const_spec((1, D_FF)),
            _resident_spec((D_FF, D_MODEL)),
            _const_spec((1, D_MODEL)),
        ],
        out_specs=pl.BlockSpec((tm, D_MODEL), lambda i: (i, 0)),
        out_shape=jax.ShapeDtypeStruct((n, D_MODEL), F32),
        compiler_params=_cparams("parallel"),
        name="ffn",
    )(x, x, x, g, wu, cw, cb, wd, gf)


def _rope_tables(T):
    t = jnp.arange(T)

    def parts(pos, half):
        freqs = ROPE_THETA ** (-jnp.arange(half, dtype=F32) / half)
        ang = pos.astype(F32)[:, None] * freqs[None, :]
        c, s, z = jnp.cos(ang), jnp.sin(ang), jnp.zeros_like(ang)
        return jnp.concatenate([c, c], -1), jnp.concatenate([-s, z], -1), jnp.concatenate([z, s], -1)

    lin = parts(t, HEAD_DIM // 2)
    row = parts(t // GRID_W, HEAD_DIM // 4)
    colp = parts(t % GRID_W, HEAD_DIM // 4)
    ax = [jnp.concatenate([a, b], -1) for a, b in zip(row, colp)]
    return jnp.concatenate([jnp.tile(z, (1, 2)) for z in (*lin, *ax)], axis=-1)


def _na_bias_table(rpb):
    c = np.arange(GRID_W)
    qstart = np.clip(c - NA_COLS // 2, 0, GRID_W - NA_COLS)
    rel = c[None, :] - qstart[:, None]
    valid = (rel >= 0) & (rel < NA_COLS)
    dc = np.clip(c[None, :] - c[:, None], -(NA_COLS - 1), NA_COLS - 1) + (NA_COLS - 1)
    e = np.arange(NA_ROWS)
    m = np.arange(NA_ROWS)
    dr = m[None, :] - e[:, None] + (NA_ROWS - 1)
    onehot = np.zeros((2 * NA_COLS - 1, GRID_W * GRID_W), np.float32)
    onehot[dc.reshape(-1), np.arange(GRID_W * GRID_W)] = 1.0
    rows = rpb[:, dr].reshape(-1, 2 * NA_COLS - 1)
    tbl = jnp.dot(rows, onehot, precision=lax.Precision.HIGHEST)
    tbl = tbl.reshape(rpb.shape[0], NA_ROWS, NA_ROWS, GRID_W, GRID_W)
    tbl = jnp.where(valid[None, None, None], tbl, NEG)
    tbl = tbl.transpose(1, 0, 3, 2, 4)
    return tbl.reshape(NA_ROWS, 2, 2 * GRID_W, NA_ROWS * GRID_W).astype(F32)


def _gqa_heads(z, axis):
    parts = [lax.slice_in_dim(z, hd * HEAD_DIM, (hd + 1) * HEAD_DIM, axis=axis) for hd in GQA_HEAD_ORDER]
    return jnp.concatenate(parts, axis=axis)


def _trunk(x, mem, p):
    B, T, _ = x.shape
    n = B * T
    x = x.reshape(n, D_MODEL)
    tab = _rope_tables(T)
    bd = jnp.asarray(np.kron(np.eye(2), np.ones((HEAD_DIM, HEAD_DIM))), BF16)
    mem2 = mem.reshape(B * MEM_LEN, D_MODEL)
    for l in range(DEPTH):
        qa, ka, va, qb, kb, vb, qc, kc, vc, qc4, kc4, vc4, qc16, kc16, vc16, qd, kd, vd = _proj(
            x, p["norm_mix_g"][l], p["w_in"][l], tab, p["gq"][l], p["gk"][l], bd, T)
        ya = _attn_a(qa, ka, va, B, T)
        yb = _band(qb, kb, vb, nseq=B, L=T, S=BAND_BLOCK, W=BAND_BLOCK + 2 * B_RADIUS, R=B_RADIUS,
                   halo=B_RADIUS, mode="sink", extra=p["sink"][l])
        cs = []
        for (window, d), qkv in zip(C_BRANCHES, ((qc, kc, vc), (qc4, kc4, vc4), (qc16, kc16, vc16))):
            r = window // (2 * d)
            blk = min(C_BLOCK, T // d)
            qs, ks, vs = (z.reshape(n, 256) for z in qkv)
            cs.append(_band(qs, ks, vs, nseq=B * d, L=T // d, S=blk, W=blk + 2 * r, R=r, halo=r, mode="lse"))
        yd = _band(qd, kd, vd, nseq=B, L=T, S=GRID_W, W=NA_ROWS * GRID_W, R=0, halo=(NA_ROWS // 2) * GRID_W,
                   mode="bias", extra=p["na_bias"][l])
        x, qx = _mixout(x, ya, yb, cs[0], cs[1], cs[2], yd, p["grp_g"][l], p["w_out"][l],
                        p["norm_x_g"][l], p["w_xq"][l], T)
        kv = _memkv(mem2, p["norm_mem_g"][l], p["w_xkv"][l])
        x = _cross(qx, kv, x, p["w_xo"][l], T)
        x = _ffn(x, p["norm_ffn_g"][l], p["w_up"][l], p["conv_w"][l], p["conv_b"][l], p["w_down"][l],
                 p["final_g"], T, final=(l == DEPTH - 1))
    return x.reshape(B, T, D_MODEL)


def _prepare(norm_mix_g, w_in, a_q_norm_g, a_k_norm_g, b_sink, d_rpb, grp_norm_g, w_out, norm_x_g,
             norm_mem_g, w_xq, w_xkv, w_xo, norm_ffn_g, w_up, conv_w, conv_b, w_down, final_norm_g):
    layers = range(norm_mix_g.shape[0])

    def w_in_layer(w):
        return jnp.concatenate([_gqa_heads(w[:, 0:256], 1), w[:, 256:512], _gqa_heads(w[:, 512:768], 1),
                                w[:, 768:]], axis=1).astype(BF16)

    def w_out_layer(w):
        return jnp.concatenate([_gqa_heads(w[0:256], 0), _gqa_heads(w[256:512], 0), w[512:]], axis=0).astype(BF16)

    def grp_layer(g):
        return jnp.concatenate([_gqa_heads(g[0:2], 1), g[2:4]], axis=0)

    return {
        "norm_mix_g": [norm_mix_g[l][None, :] for l in layers],
        "w_in": [w_in_layer(w_in[l]) for l in layers],
        "gq": [jnp.tile(a_q_norm_g[l], 2)[None, :] * (HEAD_DIM ** -0.5 * LOG2E) for l in layers],
        "gk": [jnp.tile(a_k_norm_g[l], 2)[None, :] for l in layers],
        "sink": [b_sink[l][np.array(GQA_HEAD_ORDER)].reshape(2, 2) * LOG2E for l in layers],
        "na_bias": [_na_bias_table(d_rpb[l] * LOG2E) for l in layers],
        "grp_g": [grp_layer(grp_norm_g[l]) for l in layers],
        "w_out": [w_out_layer(w_out[l]) for l in layers],
        "norm_x_g": [norm_x_g[l][None, :] for l in layers],
        "norm_mem_g": [norm_mem_g[l][None, :] for l in layers],
        "w_xq": [w_xq[l].astype(BF16) for l in layers],
        "w_xkv": [w_xkv[l].astype(BF16) for l in layers],
        "w_xo": [w_xo[l].astype(BF16) for l in layers],
        "norm_ffn_g": [norm_ffn_g[l][None, :] for l in layers],
        "w_up": [w_up[l].astype(BF16) for l in layers],
        "conv_w": [conv_w[l] for l in layers],
        "conv_b": [conv_b[l][None, :] for l in layers],
        "w_down": [w_down[l].astype(BF16) for l in layers],
        "final_g": final_norm_g[None, :],
    }


def kernel(x_prompt, x_sample, mem_prompt, mem_sample, norm_mix_g, w_in, a_q_norm_g, a_k_norm_g, b_sink,
           d_rpb, grp_norm_g, w_out, norm_x_g, norm_mem_g, w_xq, w_xkv, w_xo, norm_ffn_g, w_up, conv_w,
           conv_b, w_down, final_norm_g):
    p = _prepare(norm_mix_g, w_in, a_q_norm_g, a_k_norm_g, b_sink, d_rpb, grp_norm_g, w_out, norm_x_g,
                 norm_mem_g, w_xq, w_xkv, w_xo, norm_ffn_g, w_up, conv_w, conv_b, w_down, final_norm_g)
    return _trunk(x_prompt, mem_prompt, p), _trunk(x_sample, mem_sample, p)
```

```python
import functools
import math

import numpy as np
import jax
import jax.numpy as jnp
from jax import lax
from jax.experimental import pallas as pl
from jax.experimental.pallas import tpu as pltpu

F32 = jnp.float32
BF16 = jnp.bfloat16

D_MODEL = 1024
DEPTH = 4
HEAD_DIM = 64
GROUP_WIDTH = 256
GRID_W = 64
ROPE_THETA = 10000.0
B_RADIUS = 128
C_BRANCHES = ((128, 1), (512, 4), (2048, 16))
NA_ROWS = 8
NA_COLS = 16
MEM_LEN = 256
X_HEADS = 4
X_HEAD_DIM = D_MODEL // X_HEADS
D_FF = 2816
EPS = 1e-6
NEG = -1e30
IN_WIDTH = 2560

LANES = 128
VMEM_LIMIT = 56 * 1024 * 1024
TOKEN_TILE = 1024
FFN_TILE = 512
BAND_TOKENS = 2048
A_TQ = 512
A_TQ_ONE_BLOCK = 1024
A_TK = 8192
A_ROWS = 128
BAND_BLOCK = 256
C_BLOCK = 128
BAND_UNROLL_ROWS = 1024
LOG2E = 1.4426950408889634
MXU_COLS = 256
FF_CHUNKS = ((0, 6 * MXU_COLS), (6 * MXU_COLS, D_FF))

GQA_HEAD_ORDER = (0, 2, 1, 3)
NT_DIMS = (((1,), (1,)), ((), ()))


def _cparams(*sem):
    return pltpu.CompilerParams(dimension_semantics=sem, vmem_limit_bytes=VMEM_LIMIT)


def _const_spec(shape):
    n = len(shape)
    return pl.BlockSpec(shape, lambda *_: (0,) * n)


def _resident_spec(shape):
    n = len(shape)
    return pl.BlockSpec(shape, lambda *_: (0,) * n, pipeline_mode=pl.Buffered(1))


def _rms_rows(x, g):
    return x * lax.rsqrt(jnp.mean(x * x, axis=-1, keepdims=True) + EPS) * g


def _low_lanes():
    return lax.broadcasted_iota(jnp.int32, (1, LANES), 1) < HEAD_DIM


def _first_head_lanes(rotary):
    lane = lax.broadcasted_iota(jnp.int32, (1, LANES), 1)
    return (lane % HEAD_DIM) < HEAD_DIM // 2 if rotary else lane < HEAD_DIM


def _stack_heads(qb, low):
    zero = jnp.zeros_like(qb)
    return jnp.concatenate([jnp.where(low, qb, zero), jnp.where(low, zero, qb)], axis=0)


def _rope(y, c, s):
    return y * c + pltpu.roll(y, HEAD_DIM, 1) * s


def _proj_kernel(x_ref, g_ref, w_ref, tab_ref, gq_ref, gk_ref, bd_ref,
                 qa, ka, va, qb, kb, vb, qc, kc, vc, qc4, kc4, vc4, qc16, kc16, vc16, qd, kd, vd,
                 q_scr, k_scr, v_scr):
    tm = x_ref.shape[0]
    scale = HEAD_DIM ** -0.5 * LOG2E
    h = _rms_rows(x_ref[...], g_ref[...]).astype(BF16)

    def halves(y):
        return y[:, :LANES], y[:, LANES:]

    def tab(k):
        return tab_ref[:, k * LANES:(k + 1) * LANES]

    def rope_lin(y):
        return _rope(y, tab(0), tab(1))

    def rope_ax(y):
        return _rope(y, tab(2), tab(3))

    def qknorm(y, g):
        ss = jnp.dot((y * y).astype(BF16), bd_ref[...], preferred_element_type=F32) * (1.0 / HEAD_DIM)
        return y * lax.rsqrt(ss + EPS) * g

    def emit_c(scr, y0, y1, nat, subs):
        for hb, y in enumerate((y0, y1)):
            cols = slice(hb * LANES, (hb + 1) * LANES)
            scr[hb] = y
            nat[:, cols] = y.astype(BF16)
            for (_, d), ref in zip(C_BRANCHES[1:], subs):
                for r in range(d):
                    ref[0, r, :, cols] = scr[hb, pl.ds(r, tm // d, stride=d), :].astype(BF16)

    ys = [halves(jnp.dot(h, w_ref[:, c0:c0 + 256], preferred_element_type=F32))
          for c0 in range(0, IN_WIDTH, 256)]

    y0, y1 = ys[0]
    qa[:, :LANES] = rope_ax(qknorm(y0, gq_ref[...])).astype(BF16)
    qa[:, LANES:] = rope_ax(qknorm(y1, gq_ref[...])).astype(BF16)
    y0, y1 = ys[1]
    ka[...] = rope_ax(qknorm(y0, gk_ref[...])).astype(BF16)
    va[...] = y1.astype(BF16)

    y0, y1 = ys[2]
    qb[:, :LANES] = rope_lin(y0 * scale).astype(BF16)
    qb[:, LANES:] = rope_lin(y1 * scale).astype(BF16)
    y0, y1 = ys[3]
    kb[...] = rope_lin(y0).astype(BF16)
    vb[...] = y1.astype(BF16)

    y0, y1 = ys[4]
    emit_c(q_scr, rope_lin(y0 * scale), rope_lin(y1 * scale), qc, (qc4, qc16))
    y0, y1 = ys[5]
    emit_c(k_scr, rope_lin(y0), rope_lin(y1), kc, (kc4, kc16))
    y0, y1 = ys[6]
    emit_c(v_scr, y0, y1, vc, (vc4, vc16))

    y0, y1 = ys[7]
    qd[:, :LANES] = (y0 * scale).astype(BF16)
    qd[:, LANES:] = (y1 * scale).astype(BF16)
    for (y0, y1), ref in zip(ys[8:], (kd, vd)):
        ref[:, :LANES] = y0.astype(BF16)
        ref[:, LANES:] = y1.astype(BF16)


def _proj(x, g, w, tab, gq, gk, bd, T):
    n = x.shape[0]
    tm = TOKEN_TILE
    tpb = T // tm
    B = n // T
    row = lambda wd: (pl.BlockSpec((tm, wd), lambda i: (i, 0)), jax.ShapeDtypeStruct((n, wd), BF16))

    def slab(d):
        return (pl.BlockSpec((1, d, tm // d, 256), lambda i: (i // tpb, 0, i % tpb, 0)),
                jax.ShapeDtypeStruct((B, d, T // d, 256), BF16))

    outs = ([row(256), row(128), row(128)] * 2 + [row(256)] * 3
            + [slab(d) for _, d in C_BRANCHES[1:] for _ in range(3)] + [row(256)] * 3)
    return pl.pallas_call(
        _proj_kernel,
        grid=(n // tm,),
        in_specs=[
            pl.BlockSpec((tm, D_MODEL), lambda i: (i, 0)),
            _const_spec((1, D_MODEL)),
            _resident_spec((D_MODEL, IN_WIDTH)),
            pl.BlockSpec((tm, 4 * LANES), lambda i: (i % tpb, 0)),
            _const_spec((1, LANES)),
            _const_spec((1, LANES)),
            _const_spec((LANES, LANES)),
        ],
        out_specs=[o[0] for o in outs],
        out_shape=[o[1] for o in outs],
        scratch_shapes=[pltpu.VMEM((2, tm, LANES), F32)] * 3,
        compiler_params=_cparams("parallel"),
        name="proj",
    )(x, g, w, tab, gq, gk, bd)


def _attn_a_kernel(q_ref, k_ref, v_ref, o_ref, m_sc, acc_sc, *, nk):
    j = pl.program_id(2)
    tq = q_ref.shape[0]
    tk = k_ref.shape[0]

    @pl.when(j == 0)
    def _():
        m_sc[...] = jnp.full(m_sc.shape, NEG, F32)
        acc_sc[...] = jnp.zeros(acc_sc.shape, F32)

    low = _low_lanes()
    qfirst = _first_head_lanes(rotary=True)
    k = k_ref[...]
    v1 = jnp.concatenate([v_ref[...], jnp.ones((tk, LANES), BF16)], axis=1)
    for jb in range(2):
        q2 = _stack_heads(q_ref[:, jb * LANES:(jb + 1) * LANES], qfirst)
        for rb in range(2 * tq // A_ROWS):
            rows = slice(rb * A_ROWS, (rb + 1) * A_ROWS)
            s = lax.dot_general(q2[rows], k, NT_DIMS, preferred_element_type=F32)
            m = m_sc[jb, rows, :]
            m_new = jnp.maximum(m, jnp.max(s, axis=-1, keepdims=True))
            p = jnp.exp2(s - m_new).astype(BF16)
            acc_sc[jb, rows, :] = (acc_sc[jb, rows, :] * jnp.exp2(m - m_new)
                                   + jnp.dot(p, v1, preferred_element_type=F32))
            m_sc[jb, rows, :] = m_new

    @pl.when(j == nk - 1)
    def _():
        for jb in range(2):
            cols = slice(jb * LANES, (jb + 1) * LANES)
            acc = acc_sc[jb]
            num = jnp.where(low, acc[:tq, :LANES], acc[tq:, :LANES])
            den = jnp.where(low, acc[:tq, LANES:], acc[tq:, LANES:])
            o_ref[:, cols] = (num / den).astype(o_ref.dtype)


def _attn_a(q, k, v, B, T):
    n = q.shape[0]
    tk = min(A_TK, T)
    tq = A_TQ_ONE_BLOCK if tk == T else A_TQ
    nq, nk = T // tq, T // tk
    return pl.pallas_call(
        functools.partial(_attn_a_kernel, nk=nk),
        grid=(B, nq, nk),
        in_specs=[
            pl.BlockSpec((tq, 256), lambda b, i, j: (b * nq + i, 0)),
            pl.BlockSpec((tk, LANES), lambda b, i, j: (b * nk + j, 0)),
            pl.BlockSpec((tk, LANES), lambda b, i, j: (b * nk + j, 0)),
        ],
        out_specs=pl.BlockSpec((tq, 256), lambda b, i, j: (b * nq + i, 0)),
        out_shape=jax.ShapeDtypeStruct((n, 256), BF16),
        scratch_shapes=[
            pltpu.VMEM((2, 2 * tq, 1), F32),
            pltpu.VMEM((2, 2 * tq, 256), F32),
        ],
        compiler_params=_cparams("parallel", "parallel", "arbitrary"),
        name="attn_a",
    )(q, k, v)


def _band_kernel(*refs, S, W, halo, align, L, kvw, mode, unroll):
    if mode == "sink":
        sink_ref, q_ref, k_ref, v_ref, tab_ref, o_ref = refs
    elif mode == "lse":
        q_ref, k_ref, v_ref, tab_ref, o_ref, lse_ref = refs
    else:
        q_ref, k_ref, v_ref, tab_ref, o_ref = refs
    qi = pl.program_id(1)
    G, TQ = q_ref.shape[0], q_ref.shape[1]
    per = TQ // S
    low = _low_lanes()
    qfirst = _first_head_lanes(rotary=mode != "bias")
    ones = jnp.ones((W, LANES), BF16)
    first_half = lax.broadcasted_iota(jnp.int32, (2 * S, 1), 0) < S

    def body(it, carry):
        g = it // per
        i = it % per
        ig = qi * per + i
        start = pl.multiple_of(jnp.clip(ig * S - halo, 0, L - W), align)
        q0 = pl.multiple_of(i * S, S)
        qs = q_ref[g, pl.ds(q0, S), :]
        ks = k_ref[g, pl.ds(start, W), :]
        vs = v_ref[g, pl.ds(start, W), :]
        case = (ig * S - start) // (S if mode == "bias" else halo)
        for jb in range(2):
            cols = slice(jb * LANES, (jb + 1) * LANES)
            kcols = cols if kvw == 256 else slice(0, LANES)
            q2 = _stack_heads(qs[:, cols], qfirst)
            s = lax.dot_general(q2, ks[:, kcols], NT_DIMS, preferred_element_type=F32)
            s = s + (tab_ref[case, jb] if mode == "bias" else tab_ref[case])
            m = jnp.max(s, axis=-1, keepdims=True)
            if mode == "sink":
                sk = jnp.where(first_half, sink_ref[jb, 0], sink_ref[jb, 1])
                m = jnp.maximum(m, sk)
            p = jnp.exp2(s - m).astype(BF16)
            v1 = jnp.concatenate([vs[:, kcols], ones], axis=1)
            pv = jnp.dot(p, v1, preferred_element_type=F32)
            l = pv[:, LANES:]
            if mode == "sink":
                l = l + jnp.exp2(sk - m)
            num = jnp.where(low, pv[:S, :LANES], pv[S:, :LANES])
            den = jnp.where(low, l[:S], l[S:])
            o_ref[g, pl.ds(q0, S), cols] = (num / den).astype(o_ref.dtype)
            if mode == "lse":
                lse_ref[g, pl.ds(q0, S), cols] = (jnp.where(low, m[:S], m[S:]) + jnp.log2(den)) * (1.0 / LOG2E)
        return carry

    lax.fori_loop(0, G * per, body, 0, unroll=unroll)


def _band_mask_table(S, W, R, halo):
    rowq = np.arange(2 * S) % S
    col = np.arange(W)
    tabs = [np.where(np.abs(col[None, :] - rowq[:, None] - c * halo) <= R, 0.0, NEG) for c in range(3)]
    return jnp.asarray(np.stack(tabs), F32)


def _band(q, k, v, *, nseq, L, S, W, R, halo, mode, extra=None):
    kvw = k.shape[1]
    W = min(W, L)
    TQ = min(L, BAND_TOKENS)
    G = max(1, BAND_TOKENS // L)
    S = min(S, L)
    align = math.gcd(S, halo, L - W) if L > W else S
    unroll = max(1, BAND_UNROLL_ROWS // S)
    assert nseq % G == 0 and L % TQ == 0 and TQ % S == 0 and align % 16 == 0
    q3 = q.reshape(nseq, L, 256)
    k3 = k.reshape(nseq, L, kvw)
    v3 = v.reshape(nseq, L, kvw)
    qspec = pl.BlockSpec((G, TQ, 256), lambda s, i: (s, i, 0))
    kvspec = pl.BlockSpec((G, L, kvw), lambda s, i: (s, 0, 0))
    in_specs = [qspec, kvspec, kvspec]
    args = [q3, k3, v3]
    out_specs = qspec
    out_shape = jax.ShapeDtypeStruct((nseq, L, 256), BF16)
    tab = extra if mode == "bias" else _band_mask_table(S, W, R, halo)
    in_specs = in_specs + [_resident_spec(tab.shape)]
    args = args + [tab]
    if mode == "sink":
        in_specs = [pl.BlockSpec(memory_space=pltpu.SMEM)] + in_specs
        args = [extra] + args
    elif mode == "lse":
        out_specs = [qspec, qspec]
        out_shape = [out_shape, jax.ShapeDtypeStruct((nseq, L, 256), F32)]
    out = pl.pallas_call(
        functools.partial(_band_kernel, S=S, W=W, halo=halo, align=align, L=L, kvw=kvw, mode=mode,
                          unroll=unroll),
        grid=(nseq // G, L // TQ),
        in_specs=in_specs,
        out_specs=out_specs,
        out_shape=out_shape,
        compiler_params=_cparams("parallel", "arbitrary"),
        name="band_" + mode,
    )(*args)
    if mode == "lse":
        return out[0].reshape(nseq * L, 256), out[1].reshape(nseq * L, 256)
    return out.reshape(nseq * L, 256)


def _mixout_kernel(x_ref, ya_ref, yb_ref, c1o, c1l, c2o, c2l, c3o, c3l, yd_ref,
                   gg_ref, wo_ref, gx_ref, wq_ref, xo_ref, qx_ref, *scr):
    tm = x_ref.shape[0]

    def gn(y, gi):
        return _rms_rows(y.astype(F32), gg_ref[gi:gi + 1, :]).astype(BF16)

    def token_order(ref, buf):
        d = ref.shape[1]
        for hb in range(2):
            for r in range(d):
                buf[hb, pl.ds(r, tm // d, stride=d), :] = ref[0, r, :, hb * LANES:(hb + 1) * LANES].astype(F32)
        return jnp.concatenate([buf[0], buf[1]], axis=-1)

    o1, l1 = c1o[...].astype(F32), c1l[...]
    o2, l2 = token_order(c2o, scr[0]), token_order(c2l, scr[1])
    o3, l3 = token_order(c3o, scr[2]), token_order(c3l, scr[3])
    mx = jnp.maximum(jnp.maximum(l1, l2), l3)
    e1, e2, e3 = jnp.exp(l1 - mx), jnp.exp(l2 - mx), jnp.exp(l3 - mx)
    yc = (e1 * o1 + e2 * o2 + e3 * o3) / (e1 + e2 + e3)
    ycat = jnp.concatenate([gn(ya_ref[...], 0), gn(yb_ref[...], 1), gn(yc, 2), gn(yd_ref[...], 3)], axis=-1)
    xn = x_ref[...] + jnp.dot(ycat, wo_ref[...], preferred_element_type=F32)
    xo_ref[...] = xn
    hq = _rms_rows(xn, gx_ref[...]).astype(BF16)
    qx_ref[...] = (jnp.dot(hq, wq_ref[...], preferred_element_type=F32) * (X_HEAD_DIM ** -0.5)).astype(BF16)


def _mixout(x, ya, yb, c1, c2, c3, yd, gg, wo, gx, wq, T):
    n = x.shape[0]
    B = n // T
    tm = TOKEN_TILE
    tpb = T // tm
    row = lambda wd: pl.BlockSpec((tm, wd), lambda i: (i, 0))
    slab = lambda d: pl.BlockSpec((1, d, tm // d, 256), lambda i: (i // tpb, 0, i % tpb, 0))
    d2, d3 = C_BRANCHES[1][1], C_BRANCHES[2][1]
    return pl.pallas_call(
        _mixout_kernel,
        grid=(n // tm,),
        in_specs=[row(D_MODEL)] + [row(256)] * 4 + [slab(d2)] * 2 + [slab(d3)] * 2 + [row(256)] + [
            _const_spec((4, GROUP_WIDTH)),
            _resident_spec((D_MODEL, D_MODEL)),
            _const_spec((1, D_MODEL)),
            _resident_spec((D_MODEL, D_MODEL)),
        ],
        out_specs=[row(D_MODEL), row(D_MODEL)],
        out_shape=[jax.ShapeDtypeStruct((n, D_MODEL), F32), jax.ShapeDtypeStruct((n, D_MODEL), BF16)],
        scratch_shapes=[pltpu.VMEM((2, tm, LANES), F32)] * 4,
        compiler_params=_cparams("parallel"),
        name="mixout",
    )(x, ya, yb, c1[0], c1[1], *[z.reshape(B, d2, T // d2, 256) for z in c2],
      *[z.reshape(B, d3, T // d3, 256) for z in c3], yd, gg, wo, gx, wq)


def _memkv_kernel(m_ref, g_ref, w_ref, kv_ref):
    h = _rms_rows(m_ref[...], g_ref[...]).astype(BF16)
    kv_ref[...] = jnp.dot(h, w_ref[...], preferred_element_type=F32).astype(BF16)


def _memkv(mem, g, w):
    n = mem.shape[0]
    return pl.pallas_call(
        _memkv_kernel,
        grid=(n // MEM_LEN,),
        in_specs=[
            pl.BlockSpec((MEM_LEN, D_MODEL), lambda i: (i, 0)),
            _const_spec((1, D_MODEL)),
            _resident_spec((D_MODEL, 2 * D_MODEL)),
        ],
        out_specs=pl.BlockSpec((MEM_LEN, 2 * D_MODEL), lambda i: (i, 0)),
        out_shape=jax.ShapeDtypeStruct((n, 2 * D_MODEL), BF16),
        compiler_params=_cparams("parallel"),
        name="memkv",
    )(mem, g, w)


def _cross_kernel(qx_ref, kv_ref, x_ref, wo_ref, xo_ref):
    outs = []
    for h in range(X_HEADS):
        cols = slice(h * X_HEAD_DIM, (h + 1) * X_HEAD_DIM)
        vcols = slice(D_MODEL + h * X_HEAD_DIM, D_MODEL + (h + 1) * X_HEAD_DIM)
        s = lax.dot_general(qx_ref[:, cols], kv_ref[:, cols], NT_DIMS, preferred_element_type=F32)
        m = jnp.max(s, axis=-1, keepdims=True)
        p = jnp.exp(s - m)
        l = jnp.sum(p, axis=-1, keepdims=True)
        o = jnp.dot(p.astype(BF16), kv_ref[:, vcols], preferred_element_type=F32) / l
        outs.append(o.astype(BF16))
    o = jnp.concatenate(outs, axis=-1)
    xo_ref[...] = x_ref[...] + jnp.dot(o, wo_ref[...], preferred_element_type=F32)


def _cross(qx, kv, x, wo, T):
    n = x.shape[0]
    tm = TOKEN_TILE
    tpb = T // tm
    row = lambda wd: pl.BlockSpec((tm, wd), lambda i: (i, 0))
    return pl.pallas_call(
        _cross_kernel,
        grid=(n // tm,),
        in_specs=[
            row(D_MODEL),
            pl.BlockSpec((MEM_LEN, 2 * D_MODEL), lambda i: (i // tpb, 0)),
            row(D_MODEL),
            _resident_spec((D_MODEL, D_MODEL)),
        ],
        out_specs=row(D_MODEL),
        out_shape=jax.ShapeDtypeStruct((n, D_MODEL), F32),
        compiler_params=_cparams("parallel"),
        name="cross",
    )(qx, kv, x, wo)


HALO = 16


def _ffn_kernel(xp_ref, x_ref, xn_ref, g_ref, wu_ref, cw_ref, cb_ref, wd_ref, gf_ref, o_ref, *, tpb, final):
    i = pl.program_id(0)
    tm = x_ref.shape[0]
    x = x_ref[...]
    xe = jnp.concatenate([xp_ref[...], x, xn_ref[...]], axis=0)
    he = _rms_rows(xe, g_ref[...]).astype(BF16)
    hc = he[HALO:HALO + tm]
    rows = lax.broadcasted_iota(jnp.int32, (tm + 2 * HALO, 1), 0)
    seq_first = (i % tpb) == 0
    seq_last = (i % tpb) == tpb - 1
    keep = jnp.logical_and(jnp.logical_or(rows >= HALO, jnp.logical_not(seq_first)),
                           jnp.logical_or(rows < HALO + tm, jnp.logical_not(seq_last)))
    keep = keep.astype(F32)
    acc = x
    for c0, c1 in FF_CHUNKS:
        gc = slice(c0, c1)
        uc = slice(D_FF + c0, D_FF + c1)
        ge = jnp.dot(he, wu_ref[:, gc], preferred_element_type=F32) * keep
        val = jnp.dot(hc, wu_ref[:, uc], preferred_element_type=F32)
        gate = (ge[HALO - 1:HALO - 1 + tm] * cw_ref[0:1, gc] + ge[HALO:HALO + tm] * cw_ref[1:2, gc]
                + ge[HALO + 1:HALO + 1 + tm] * cw_ref[2:3, gc] + cb_ref[:, gc])
        act = 0.5 * gate * (1.0 + lax.erf(gate * (2.0 ** -0.5))) * val
        acc = acc + jnp.dot(act.astype(BF16), wd_ref[gc, :], preferred_element_type=F32)
    if final:
        acc = _rms_rows(acc, gf_ref[...])
    o_ref[...] = acc


def _ffn(x, g, wu, cw, cb, wd, gf, T, final):
    n = x.shape[0]
    tm = FFN_TILE
    tpb = T // tm
    hb = tm // HALO
    nhb = n // HALO
    return pl.pallas_call(
        functools.partial(_ffn_kernel, tpb=tpb, final=final),
        grid=(n // tm,),
        in_specs=[
            pl.BlockSpec((HALO, D_MODEL), lambda i: (jnp.maximum(i * hb - 1, 0), 0)),
            pl.BlockSpec((tm, D_MODEL), lambda i: (i, 0)),
            pl.BlockSpec((HALO, D_MODEL), lambda i: (jnp.minimum((i + 1) * hb, nhb - 1), 0)),
            _const_spec((1, D_MODEL)),
            _resident_spec((D_MODEL, 2 * D_FF)),
            _const_spec((3, D_FF)),
            _const_spec((1, D_FF)),
            _resident_spec((D_FF, D_MODEL)),
            _const_spec((1, D_MODEL)),
        ],
        out_specs=pl.BlockSpec((tm, D_MODEL), lambda i: (i, 0)),
        out_shape=jax.ShapeDtypeStruct((n, D_MODEL), F32),
        compiler_params=_cparams("parallel"),
        name="ffn",
    )(x, x, x, g, wu, cw, cb, wd, gf)


def _rope_tables(T):
    t = jnp.arange(T)

    def cos_sin(pos, half):
        freqs = ROPE_THETA ** (-jnp.arange(half, dtype=F32) / half)
        ang = pos.astype(F32)[:, None] * freqs[None, :]
        return jnp.cos(ang), jnp.sin(ang)

    def lanes(c, s):
        return jnp.tile(c, (1, 4)), jnp.concatenate([-jnp.tile(s, (1, 2)), jnp.tile(s, (1, 2))], axis=-1)

    lin = lanes(*cos_sin(t, HEAD_DIM // 2))
    cr, sr = cos_sin(t // GRID_W, HEAD_DIM // 4)
    cc, sc = cos_sin(t % GRID_W, HEAD_DIM // 4)
    ax = lanes(jnp.concatenate([cr, cc], -1), jnp.concatenate([sr, sc], -1))
    return jnp.concatenate([*lin, *ax], axis=-1)


def _na_bias_table(rpb):
    c = np.arange(GRID_W)
    qstart = np.clip(c - NA_COLS // 2, 0, GRID_W - NA_COLS)
    rel = c[None, :] - qstart[:, None]
    valid = (rel >= 0) & (rel < NA_COLS)
    dc = np.clip(c[None, :] - c[:, None], -(NA_COLS - 1), NA_COLS - 1) + (NA_COLS - 1)
    e = np.arange(NA_ROWS)
    m = np.arange(NA_ROWS)
    dr = m[None, :] - e[:, None] + (NA_ROWS - 1)
    onehot = np.zeros((2 * NA_COLS - 1, GRID_W * GRID_W), np.float32)
    onehot[dc.reshape(-1), np.arange(GRID_W * GRID_W)] = 1.0
    rows = rpb[:, dr].reshape(-1, 2 * NA_COLS - 1)
    tbl = jnp.dot(rows, onehot, precision=lax.Precision.HIGHEST)
    tbl = tbl.reshape(rpb.shape[0], NA_ROWS, NA_ROWS, GRID_W, GRID_W)
    tbl = jnp.where(valid[None, None, None], tbl, NEG)
    tbl = tbl.transpose(1, 0, 3, 2, 4)
    return tbl.reshape(NA_ROWS, 2, 2 * GRID_W, NA_ROWS * GRID_W).astype(F32)


def _rotary_lanes(z, axial):
    lead = z.shape[:-1]
    nb = z.shape[-1] // LANES
    if axial:
        z = z.reshape(*lead, nb, 2, 2, 2, HEAD_DIM // 4)
        z = jnp.moveaxis(z, -2, -4)
    else:
        z = z.reshape(*lead, nb, 2, 2, HEAD_DIM // 2)
        z = jnp.swapaxes(z, -2, -3)
    return z.reshape(*lead, nb * LANES)


def _gqa_heads(z, axis):
    parts = [lax.slice_in_dim(z, hd * HEAD_DIM, (hd + 1) * HEAD_DIM, axis=axis) for hd in GQA_HEAD_ORDER]
    return jnp.concatenate(parts, axis=axis)


def _trunk(x, mem, p):
    B, T, _ = x.shape
    n = B * T
    x = x.reshape(n, D_MODEL)
    tab = _rope_tables(T)
    head_of_lane = (np.arange(LANES) % HEAD_DIM) // (HEAD_DIM // 2)
    bd = jnp.asarray(head_of_lane[:, None] == head_of_lane[None, :], BF16)
    mem2 = mem.reshape(B * MEM_LEN, D_MODEL)
    for l in range(DEPTH):
        qa, ka, va, qb, kb, vb, qc, kc, vc, qc4, kc4, vc4, qc16, kc16, vc16, qd, kd, vd = _proj(
            x, p["norm_mix_g"][l], p["w_in"][l], tab, p["gq"][l], p["gk"][l], bd, T)
        ya = _attn_a(qa, ka, va, B, T)
        yb = _band(qb, kb, vb, nseq=B, L=T, S=BAND_BLOCK, W=BAND_BLOCK + 2 * B_RADIUS, R=B_RADIUS,
                   halo=B_RADIUS, mode="sink", extra=p["sink"][l])
        cs = []
        for (window, d), qkv in zip(C_BRANCHES, ((qc, kc, vc), (qc4, kc4, vc4), (qc16, kc16, vc16))):
            r = window // (2 * d)
            blk = min(C_BLOCK, T // d)
            qs, ks, vs = (z.reshape(n, 256) for z in qkv)
            cs.append(_band(qs, ks, vs, nseq=B * d, L=T // d, S=blk, W=blk + 2 * r, R=r, halo=r, mode="lse"))
        yd = _band(qd, kd, vd, nseq=B, L=T, S=GRID_W, W=NA_ROWS * GRID_W, R=0, halo=(NA_ROWS // 2) * GRID_W,
                   mode="bias", extra=p["na_bias"][l])
        x, qx = _mixout(x, ya, yb, cs[0], cs[1], cs[2], yd, p["grp_g"][l], p["w_out"][l],
                        p["norm_x_g"][l], p["w_xq"][l], T)
        kv = _memkv(mem2, p["norm_mem_g"][l], p["w_xkv"][l])
        x = _cross(qx, kv, x, p["w_xo"][l], T)
        x = _ffn(x, p["norm_ffn_g"][l], p["w_up"][l], p["conv_w"][l], p["conv_b"][l], p["w_down"][l],
                 p["final_g"], T, final=(l == DEPTH - 1))
    return x.reshape(B, T, D_MODEL)


def _prepare(norm_mix_g, w_in, a_q_norm_g, a_k_norm_g, b_sink, d_rpb, grp_norm_g, w_out, norm_x_g,
             norm_mem_g, w_xq, w_xkv, w_xo, norm_ffn_g, w_up, conv_w, conv_b, w_down, final_norm_g):
    layers = range(norm_mix_g.shape[0])

    def w_in_layer(w):
        return jnp.concatenate([
            _rotary_lanes(_gqa_heads(w[:, 0:256], 1), True), _rotary_lanes(w[:, 256:384], True), w[:, 384:512],
            _rotary_lanes(_gqa_heads(w[:, 512:768], 1), False), _rotary_lanes(w[:, 768:896], False),
            w[:, 896:1024], _rotary_lanes(w[:, 1024:1536], False), w[:, 1536:]], axis=1).astype(BF16)

    def w_out_layer(w):
        return jnp.concatenate([_gqa_heads(w[0:256], 0), _gqa_heads(w[256:512], 0), w[512:]], axis=0).astype(BF16)

    def grp_layer(g):
        return jnp.concatenate([_gqa_heads(g[0:2], 1), g[2:4]], axis=0)

    return {
        "norm_mix_g": [norm_mix_g[l][None, :] for l in layers],
        "w_in": [w_in_layer(w_in[l]) for l in layers],
        "gq": [_rotary_lanes(jnp.tile(a_q_norm_g[l], 2)[None, :], True) * (HEAD_DIM ** -0.5 * LOG2E)
               for l in layers],
        "gk": [_rotary_lanes(jnp.tile(a_k_norm_g[l], 2)[None, :], True) for l in layers],
        "sink": [b_sink[l][np.array(GQA_HEAD_ORDER)].reshape(2, 2) * LOG2E for l in layers],
        "na_bias": [_na_bias_table(d_rpb[l] * LOG2E) for l in layers],
        "grp_g": [grp_layer(grp_norm_g[l]) for l in layers],
        "w_out": [w_out_layer(w_out[l]) for l in layers],
        "norm_x_g": [norm_x_g[l][None, :] for l in layers],
        "norm_mem_g": [norm_mem_g[l][None, :] for l in layers],
        "w_xq": [w_xq[l].astype(BF16) for l in layers],
        "w_xkv": [w_xkv[l].astype(BF16) for l in layers],
        "w_xo": [w_xo[l].astype(BF16) for l in layers],
        "norm_ffn_g": [norm_ffn_g[l][None, :] for l in layers],
        "w_up": [w_up[l].astype(BF16) for l in layers],
        "conv_w": [conv_w[l] for l in layers],
        "conv_b": [conv_b[l][None, :] for l in layers],
        "w_down": [w_down[l].astype(BF16) for l in layers],
        "final_g": final_norm_g[None, :],
    }


def kernel(x_prompt, x_sample, mem_prompt, mem_sample, norm_mix_g, w_in, a_q_norm_g, a_k_norm_g, b_sink,
           d_rpb, grp_norm_g, w_out, norm_x_g, norm_mem_g, w_xq, w_xkv, w_xo, norm_ffn_g, w_up, conv_w,
           conv_b, w_down, final_norm_g):
    p = _prepare(norm_mix_g, w_in, a_q_norm_g, a_k_norm_g, b_sink, d_rpb, grp_norm_g, w_out, norm_x_g,
                 norm_mem_g, w_xq, w_xkv, w_xo, norm_ffn_g, w_up, conv_w, conv_b, w_down, final_norm_g)
    return _trunk(x_prompt, mem_prompt, p), _trunk(x_sample, mem_sample, p)
```

```python
import functools
import math

import numpy as np
import jax
import jax.numpy as jnp
from jax import lax
from jax.experimental import pallas as pl
from jax.experimental.pallas import tpu as pltpu

F32 = jnp.float32
BF16 = jnp.bfloat16

D_MODEL = 1024
DEPTH = 4
HEAD_DIM = 64
GROUP_WIDTH = 256
GRID_W = 64
ROPE_THETA = 10000.0
B_RADIUS = 128
C_BRANCHES = ((128, 1), (512, 4), (2048, 16))
NA_ROWS = 8
NA_COLS = 16
MEM_LEN = 256
X_HEADS = 4
X_HEAD_DIM = D_MODEL // X_HEADS
D_FF = 2816
EPS = 1e-6
NEG = -1e30
IN_WIDTH = 2560

LANES = 128
VMEM_LIMIT = 56 * 1024 * 1024
TOKEN_TILE = 1024
FFN_TILE = 512
BAND_TOKENS = 2048
A_TQ = 512
A_TQ_ONE_BLOCK = 1024
A_TK = 8192
A_ROWS = 128
BAND_BLOCK = 256
C_BLOCK = 128
BAND_UNROLL_ROWS = 1024
LOG2E = 1.4426950408889634
MXU_COLS = 256
FF_CHUNKS = ((0, 6 * MXU_COLS), (6 * MXU_COLS, D_FF))

GQA_HEAD_ORDER = (0, 2, 1, 3)
NT_DIMS = (((1,), (1,)), ((), ()))


def _cparams(*sem):
    return pltpu.CompilerParams(dimension_semantics=sem, vmem_limit_bytes=VMEM_LIMIT)


def _const_spec(shape):
    n = len(shape)
    return pl.BlockSpec(shape, lambda *_: (0,) * n)


def _resident_spec(shape):
    n = len(shape)
    return pl.BlockSpec(shape, lambda *_: (0,) * n, pipeline_mode=pl.Buffered(1))


def _rms_rows(x, g):
    return x * lax.rsqrt(jnp.mean(x * x, axis=-1, keepdims=True) + EPS) * g


def _low_lanes():
    return lax.broadcasted_iota(jnp.int32, (1, LANES), 1) < HEAD_DIM


def _first_head_lanes(rotary):
    lane = lax.broadcasted_iota(jnp.int32, (1, LANES), 1)
    return (lane % HEAD_DIM) < HEAD_DIM // 2 if rotary else lane < HEAD_DIM


def _stack_heads(qb, low):
    zero = jnp.zeros_like(qb)
    return jnp.concatenate([jnp.where(low, qb, zero), jnp.where(low, zero, qb)], axis=0)


def _rope(y, c, s):
    return y * c + pltpu.roll(y, HEAD_DIM, 1) * s


def _proj_kernel(x_ref, g_ref, w_ref, tab_ref, gq_ref, gk_ref, bd_ref,
                 qa, ka, va, qb, kb, vb, qc, kc, vc, qc4, kc4, vc4, qc16, kc16, vc16, qd, kd, vd,
                 q_scr, k_scr, v_scr):
    tm = x_ref.shape[0]
    scale = HEAD_DIM ** -0.5 * LOG2E
    h = _rms_rows(x_ref[...], g_ref[...]).astype(BF16)

    def halves(y):
        return y[:, :LANES], y[:, LANES:]

    def tab(k):
        return tab_ref[:, k * LANES:(k + 1) * LANES]

    def rope_lin(y):
        return _rope(y, tab(0), tab(1))

    def rope_ax(y):
        return _rope(y, tab(2), tab(3))

    def qknorm(y, g):
        ss = jnp.dot((y * y).astype(BF16), bd_ref[...], preferred_element_type=F32) * (1.0 / HEAD_DIM)
        return y * lax.rsqrt(ss + EPS) * g

    def emit_c(scr, y0, y1, nat, subs):
        for hb, y in enumerate((y0, y1)):
            cols = slice(hb * LANES, (hb + 1) * LANES)
            scr[hb] = y
            nat[:, cols] = y.astype(BF16)
            for (_, d), ref in zip(C_BRANCHES[1:], subs):
                for r in range(d):
                    ref[0, r, :, cols] = scr[hb, pl.ds(r, tm // d, stride=d), :].astype(BF16)

    ys = [halves(jnp.dot(h, w_ref[:, c0:c0 + 256], preferred_element_type=F32))
          for c0 in range(0, IN_WIDTH, 256)]

    y0, y1 = ys[0]
    qa[:, :LANES] = rope_ax(qknorm(y0, gq_ref[...])).astype(BF16)
    qa[:, LANES:] = rope_ax(qknorm(y1, gq_ref[...])).astype(BF16)
    y0, y1 = ys[1]
    ka[...] = rope_ax(qknorm(y0, gk_ref[...])).astype(BF16)
    va[...] = y1.astype(BF16)

    y0, y1 = ys[2]
    qb[:, :LANES] = rope_lin(y0 * scale).astype(BF16)
    qb[:, LANES:] = rope_lin(y1 * scale).astype(BF16)
    y0, y1 = ys[3]
    kb[...] = rope_lin(y0).astype(BF16)
    vb[...] = y1.astype(BF16)

    y0, y1 = ys[4]
    emit_c(q_scr, rope_lin(y0 * scale), rope_lin(y1 * scale), qc, (qc4, qc16))
    y0, y1 = ys[5]
    emit_c(k_scr, rope_lin(y0), rope_lin(y1), kc, (kc4, kc16))
    y0, y1 = ys[6]
    emit_c(v_scr, y0, y1, vc, (vc4, vc16))

    y0, y1 = ys[7]
    qd[:, :LANES] = (y0 * scale).astype(BF16)
    qd[:, LANES:] = (y1 * scale).astype(BF16)
    for (y0, y1), ref in zip(ys[8:], (kd, vd)):
        ref[:, :LANES] = y0.astype(BF16)
        ref[:, LANES:] = y1.astype(BF16)


def _proj(x, g, w, tab, gq, gk, bd, T):
    n = x.shape[0]
    tm = TOKEN_TILE
    tpb = T // tm
    B = n // T
    row = lambda wd: (pl.BlockSpec((tm, wd), lambda i: (i, 0)), jax.ShapeDtypeStruct((n, wd), BF16))

    def slab(d):
        return (pl.BlockSpec((1, d, tm // d, 256), lambda i: (i // tpb, 0, i % tpb, 0)),
                jax.ShapeDtypeStruct((B, d, T // d, 256), BF16))

    outs = ([row(256), row(128), row(128)] * 2 + [row(256)] * 3
            + [slab(d) for _, d in C_BRANCHES[1:] for _ in range(3)] + [row(256)] * 3)
    return pl.pallas_call(
        _proj_kernel,
        grid=(n // tm,),
        in_specs=[
            pl.BlockSpec((tm, D_MODEL), lambda i: (i, 0)),
            _const_spec((1, D_MODEL)),
            _resident_spec((D_MODEL, IN_WIDTH)),
            pl.BlockSpec((tm, 4 * LANES), lambda i: (i % tpb, 0)),
            _const_spec((1, LANES)),
            _const_spec((1, LANES)),
            _const_spec((LANES, LANES)),
        ],
        out_specs=[o[0] for o in outs],
        out_shape=[o[1] for o in outs],
        scratch_shapes=[pltpu.VMEM((2, tm, LANES), F32)] * 3,
        compiler_params=_cparams("parallel"),
        name="proj",
    )(x, g, w, tab, gq, gk, bd)


def _attn_a_kernel(q_ref, k_ref, v_ref, o_ref, m_sc, acc_sc, *, nk):
    j = pl.program_id(2)
    tq = q_ref.shape[0]
    tk = k_ref.shape[0]

    @pl.when(j == 0)
    def _():
        m_sc[...] = jnp.full(m_sc.shape, NEG, F32)
        acc_sc[...] = jnp.zeros(acc_sc.shape, F32)

    low = _low_lanes()
    qfirst = _first_head_lanes(rotary=True)
    k = k_ref[...]
    v1 = jnp.concatenate([v_ref[...], jnp.ones((tk, LANES), BF16)], axis=1)
    for jb in range(2):
        q2 = _stack_heads(q_ref[:, jb * LANES:(jb + 1) * LANES], qfirst)
        for rb in range(2 * tq // A_ROWS):
            rows = slice(rb * A_ROWS, (rb + 1) * A_ROWS)
            s = lax.dot_general(q2[rows], k, NT_DIMS, preferred_element_type=F32)
            m = m_sc[jb, rows, :]
            m_new = jnp.maximum(m, jnp.max(s, axis=-1, keepdims=True))
            p = jnp.exp2(s - m_new).astype(BF16)
            acc_sc[jb, rows, :] = (acc_sc[jb, rows, :] * jnp.exp2(m - m_new)
                                   + jnp.dot(p, v1, preferred_element_type=F32))
            m_sc[jb, rows, :] = m_new

    @pl.when(j == nk - 1)
    def _():
        for jb in range(2):
            cols = slice(jb * LANES, (jb + 1) * LANES)
            acc = acc_sc[jb]
            num = jnp.where(low, acc[:tq, :LANES], acc[tq:, :LANES])
            den = jnp.where(low, acc[:tq, LANES:], acc[tq:, LANES:])
            o_ref[:, cols] = (num / den).astype(o_ref.dtype)


def _attn_a(q, k, v, B, T):
    n = q.shape[0]
    tk = min(A_TK, T)
    tq = A_TQ_ONE_BLOCK if tk == T else A_TQ
    nq, nk = T // tq, T // tk
    return pl.pallas_call(
        functools.partial(_attn_a_kernel, nk=nk),
        grid=(B, nq, nk),
        in_specs=[
            pl.BlockSpec((tq, 256), lambda b, i, j: (b * nq + i, 0)),
            pl.BlockSpec((tk, LANES), lambda b, i, j: (b * nk + j, 0)),
            pl.BlockSpec((tk, LANES), lambda b, i, j: (b * nk + j, 0)),
        ],
        out_specs=pl.BlockSpec((tq, 256), lambda b, i, j: (b * nq + i, 0)),
        out_shape=jax.ShapeDtypeStruct((n, 256), BF16),
        scratch_shapes=[
            pltpu.VMEM((2, 2 * tq, 1), F32),
            pltpu.VMEM((2, 2 * tq, 256), F32),
        ],
        compiler_params=_cparams("parallel", "parallel", "arbitrary"),
        name="attn_a",
    )(q, k, v)


def _band_kernel(*refs, S, W, halo, align, L, kvw, mode, unroll):
    if mode == "sink":
        sink_ref, q_ref, k_ref, v_ref, tab_ref, o_ref = refs
    elif mode == "lse":
        q_ref, k_ref, v_ref, tab_ref, o_ref, lse_ref = refs
    else:
        q_ref, k_ref, v_ref, tab_ref, o_ref = refs
    qi = pl.program_id(1)
    G, TQ = q_ref.shape[0], q_ref.shape[1]
    per = TQ // S
    low = _low_lanes()
    qfirst = _first_head_lanes(rotary=mode != "bias")
    ones = jnp.ones((W, LANES), BF16)
    first_half = lax.broadcasted_iota(jnp.int32, (2 * S, 1), 0) < S

    def body(it, carry):
        g = it // per
        i = it % per
        ig = qi * per + i
        start = pl.multiple_of(jnp.clip(ig * S - halo, 0, L - W), align)
        q0 = pl.multiple_of(i * S, S)
        qs = q_ref[g, pl.ds(q0, S), :]
        ks = k_ref[g, pl.ds(start, W), :]
        vs = v_ref[g, pl.ds(start, W), :]
        case = (ig * S - start) // (S if mode == "bias" else halo)
        for jb in range(2):
            cols = slice(jb * LANES, (jb + 1) * LANES)
            kcols = cols if kvw == 256 else slice(0, LANES)
            q2 = _stack_heads(qs[:, cols], qfirst)
            s = lax.dot_general(q2, ks[:, kcols], NT_DIMS, preferred_element_type=F32)
            s = s + (tab_ref[case, jb] if mode == "bias" else tab_ref[case])
            m = jnp.max(s, axis=-1, keepdims=True)
            if mode == "sink":
                sk = jnp.where(first_half, sink_ref[jb, 0], sink_ref[jb, 1])
                m = jnp.maximum(m, sk)
            p = jnp.exp2(s - m).astype(BF16)
            v1 = jnp.concatenate([vs[:, kcols], ones], axis=1)
            pv = jnp.dot(p, v1, preferred_element_type=F32)
            l = pv[:, LANES:]
            if mode == "sink":
                l = l + jnp.exp2(sk - m)
            num = jnp.where(low, pv[:S, :LANES], pv[S:, :LANES])
            den = jnp.where(low, l[:S], l[S:])
            o_ref[g, pl.ds(q0, S), cols] = (num / den).astype(o_ref.dtype)
            if mode == "lse":
                lse_ref[g, pl.ds(q0, S), cols] = (jnp.where(low, m[:S], m[S:]) + jnp.log2(den)) * (1.0 / LOG2E)
        return carry

    lax.fori_loop(0, G * per, body, 0, unroll=unroll)


def _band_mask_table(S, W, R, halo):
    rowq = np.arange(2 * S) % S
    col = np.arange(W)
    tabs = [np.where(np.abs(col[None, :] - rowq[:, None] - c * halo) <= R, 0.0, NEG) for c in range(3)]
    return jnp.asarray(np.stack(tabs), F32)


def _band(q, k, v, *, nseq, L, S, W, R, halo, mode, extra=None):
    kvw = k.shape[1]
    W = min(W, L)
    TQ = min(L, BAND_TOKENS)
    G = max(1, BAND_TOKENS // L)
    S = min(S, L)
    align = math.gcd(S, halo, L - W) if L > W else S
    unroll = max(1, BAND_UNROLL_ROWS // S)
    assert nseq % G == 0 and L % TQ == 0 and TQ % S == 0 and align % 16 == 0
    q3 = q.reshape(nseq, L, 256)
    k3 = k.reshape(nseq, L, kvw)
    v3 = v.reshape(nseq, L, kvw)
    qspec = pl.BlockSpec((G, TQ, 256), lambda s, i: (s, i, 0))
    kvspec = pl.BlockSpec((G, L, kvw), lambda s, i: (s, 0, 0))
    in_specs = [qspec, kvspec, kvspec]
    args = [q3, k3, v3]
    out_specs = qspec
    out_shape = jax.ShapeDtypeStruct((nseq, L, 256), BF16)
    tab = extra if mode == "bias" else _band_mask_table(S, W, R, halo)
    in_specs = in_specs + [_resident_spec(tab.shape)]
    args = args + [tab]
    if mode == "sink":
        in_specs = [pl.BlockSpec(memory_space=pltpu.SMEM)] + in_specs
        args = [extra] + args
    elif mode == "lse":
        out_specs = [qspec, qspec]
        out_shape = [out_shape, jax.ShapeDtypeStruct((nseq, L, 256), F32)]
    out = pl.pallas_call(
        functools.partial(_band_kernel, S=S, W=W, halo=halo, align=align, L=L, kvw=kvw, mode=mode,
                          unroll=unroll),
        grid=(nseq // G, L // TQ),
        in_specs=in_specs,
        out_specs=out_specs,
        out_shape=out_shape,
        compiler_params=_cparams("parallel", "arbitrary"),
        name="band_" + mode,
    )(*args)
    if mode == "lse":
        return out[0].reshape(nseq * L, 256), out[1].reshape(nseq * L, 256)
    return out.reshape(nseq * L, 256)


def _mixout_kernel(x_ref, ya_ref, yb_ref, c1o, c1l, c2o, c2l, c3o, c3l, yd_ref,
                   gg_ref, wo_ref, gx_ref, wq_ref, xo_ref, qx_ref, *scr):
    tm = x_ref.shape[0]

    def gn(y, gi):
        return _rms_rows(y.astype(F32), gg_ref[gi:gi + 1, :]).astype(BF16)

    def token_order(ref, buf):
        d = ref.shape[1]
        for hb in range(2):
            for r in range(d):
                buf[hb, pl.ds(r, tm // d, stride=d), :] = ref[0, r, :, hb * LANES:(hb + 1) * LANES].astype(F32)
        return jnp.concatenate([buf[0], buf[1]], axis=-1)

    o1, l1 = c1o[...].astype(F32), c1l[...]
    o2, l2 = token_order(c2o, scr[0]), token_order(c2l, scr[1])
    o3, l3 = token_order(c3o, scr[2]), token_order(c3l, scr[3])
    mx = jnp.maximum(jnp.maximum(l1, l2), l3)
    e1, e2, e3 = jnp.exp(l1 - mx), jnp.exp(l2 - mx), jnp.exp(l3 - mx)
    yc = (e1 * o1 + e2 * o2 + e3 * o3) / (e1 + e2 + e3)
    ycat = jnp.concatenate([gn(ya_ref[...], 0), gn(yb_ref[...], 1), gn(yc, 2), gn(yd_ref[...], 3)], axis=-1)
    xn = x_ref[...] + jnp.dot(ycat, wo_ref[...], preferred_element_type=F32)
    xo_ref[...] = xn
    hq = _rms_rows(xn, gx_ref[...]).astype(BF16)
    qx_ref[...] = (jnp.dot(hq, wq_ref[...], preferred_element_type=F32) * (X_HEAD_DIM ** -0.5)).astype(BF16)


def _mixout(x, ya, yb, c1, c2, c3, yd, gg, wo, gx, wq, T):
    n = x.shape[0]
    B = n // T
    tm = TOKEN_TILE
    tpb = T // tm
    row = lambda wd: pl.BlockSpec((tm, wd), lambda i: (i, 0))
    slab = lambda d: pl.BlockSpec((1, d, tm // d, 256), lambda i: (i // tpb, 0, i % tpb, 0))
    d2, d3 = C_BRANCHES[1][1], C_BRANCHES[2][1]
    return pl.pallas_call(
        _mixout_kernel,
        grid=(n // tm,),
        in_specs=[row(D_MODEL)] + [row(256)] * 4 + [slab(d2)] * 2 + [slab(d3)] * 2 + [row(256)] + [
            _const_spec((4, GROUP_WIDTH)),
            _resident_spec((D_MODEL, D_MODEL)),
            _const_spec((1, D_MODEL)),
            _resident_spec((D_MODEL, D_MODEL)),
        ],
        out_specs=[row(D_MODEL), row(D_MODEL)],
        out_shape=[jax.ShapeDtypeStruct((n, D_MODEL), F32), jax.ShapeDtypeStruct((n, D_MODEL), BF16)],
        scratch_shapes=[pltpu.VMEM((2, tm, LANES), F32)] * 4,
        compiler_params=_cparams("parallel"),
        name="mixout",
    )(x, ya, yb, c1[0], c1[1], *[z.reshape(B, d2, T // d2, 256) for z in c2],
      *[z.reshape(B, d3, T // d3, 256) for z in c3], yd, gg, wo, gx, wq)


def _memkv_kernel(m_ref, g_ref, w_ref, kv_ref):
    h = _rms_rows(m_ref[...], g_ref[...]).astype(BF16)
    kv_ref[...] = jnp.dot(h, w_ref[...], preferred_element_type=F32).astype(BF16)


def _memkv(mem, g, w):
    n = mem.shape[0]
    return pl.pallas_call(
        _memkv_kernel,
        grid=(n // MEM_LEN,),
        in_specs=[
            pl.BlockSpec((MEM_LEN, D_MODEL), lambda i: (i, 0)),
            _const_spec((1, D_MODEL)),
            _resident_spec((D_MODEL, 2 * D_MODEL)),
        ],
        out_specs=pl.BlockSpec((MEM_LEN, 2 * D_MODEL), lambda i: (i, 0)),
        out_shape=jax.ShapeDtypeStruct((n, 2 * D_MODEL), BF16),
        compiler_params=_cparams("parallel"),
        name="memkv",
    )(mem, g, w)


def _cross_kernel(qx_ref, kv_ref, x_ref, wo_ref, xo_ref):
    outs = []
    for h in range(X_HEADS):
        cols = slice(h * X_HEAD_DIM, (h + 1) * X_HEAD_DIM)
        vcols = slice(D_MODEL + h * X_HEAD_DIM, D_MODEL + (h + 1) * X_HEAD_DIM)
        s = lax.dot_general(qx_ref[:, cols], kv_ref[:, cols], NT_DIMS, preferred_element_type=F32)
        m = jnp.max(s, axis=-1, keepdims=True)
        p = jnp.exp(s - m)
        l = jnp.sum(p, axis=-1, keepdims=True)
        o = jnp.dot(p.astype(BF16), kv_ref[:, vcols], preferred_element_type=F32) / l
        outs.append(o.astype(BF16))
    o = jnp.concatenate(outs, axis=-1)
    xo_ref[...] = x_ref[...] + jnp.dot(o, wo_ref[...], preferred_element_type=F32)


def _cross(qx, kv, x, wo, T):
    n = x.shape[0]
    tm = TOKEN_TILE
    tpb = T // tm
    row = lambda wd: pl.BlockSpec((tm, wd), lambda i: (i, 0))
    return pl.pallas_call(
        _cross_kernel,
        grid=(n // tm,),
        in_specs=[
            row(D_MODEL),
            pl.BlockSpec((MEM_LEN, 2 * D_MODEL), lambda i: (i // tpb, 0)),
            row(D_MODEL),
            _resident_spec((D_MODEL, D_MODEL)),
        ],
        out_specs=row(D_MODEL),
        out_shape=jax.ShapeDtypeStruct((n, D_MODEL), F32),
        compiler_params=_cparams("parallel"),
        name="cross",
    )(qx, kv, x, wo)


HALO = 16


def _ffn_kernel(xp_ref, x_ref, xn_ref, g_ref, wu_ref, cw_ref, cb_ref, wd_ref, gf_ref, o_ref, *, tpb, final):
    i = pl.program_id(0)
    tm = x_ref.shape[0]
    x = x_ref[...]
    xe = jnp.concatenate([xp_ref[...], x, xn_ref[...]], axis=0)
    he = _rms_rows(xe, g_ref[...]).astype(BF16)
    hc = he[HALO:HALO + tm]
    rows = lax.broadcasted_iota(jnp.int32, (tm + 2 * HALO, 1), 0)
    seq_first = (i % tpb) == 0
    seq_last = (i % tpb) == tpb - 1
    keep = jnp.logical_and(jnp.logical_or(rows >= HALO, jnp.logical_not(seq_first)),
                           jnp.logical_or(rows < HALO + tm, jnp.logical_not(seq_last)))
    keep = keep.astype(F32)
    acc = x
    for c0, c1 in FF_CHUNKS:
        gc = slice(c0, c1)
        uc = slice(D_FF + c0, D_FF + c1)
        ge = jnp.dot(he, wu_ref[:, gc], preferred_element_type=F32) * keep
        val = jnp.dot(hc, wu_ref[:, uc], preferred_element_type=F32)
        gate = (ge[HALO - 1:HALO - 1 + tm] * cw_ref[0:1, gc] + ge[HALO:HALO + tm] * cw_ref[1:2, gc]
                + ge[HALO + 1:HALO + 1 + tm] * cw_ref[2:3, gc] + cb_ref[:, gc])
        act = 0.5 * gate * (1.0 + lax.erf(gate * (2.0 ** -0.5))) * val
        acc = acc + jnp.dot(act.astype(BF16), wd_ref[gc, :], preferred_element_type=F32)
    if final:
        acc = _rms_rows(acc, gf_ref[...])
    o_ref[...] = acc


def _ffn(x, g, wu, cw, cb, wd, gf, T, final):
    n = x.shape[0]
    tm = FFN_TILE
    tpb = T // tm
    hb = tm // HALO
    nhb = n // HALO
    return pl.pallas_call(
        functools.partial(_ffn_kernel, tpb=tpb, final=final),
        grid=(n // tm,),
        in_specs=[
            pl.BlockSpec((HALO, D_MODEL), lambda i: (jnp.maximum(i * hb - 1, 0), 0)),
            pl.BlockSpec((tm, D_MODEL), lambda i: (i, 0)),
            pl.BlockSpec((HALO, D_MODEL), lambda i: (jnp.minimum((i + 1) * hb, nhb - 1), 0)),
            _const_spec((1, D_MODEL)),
            _resident_spec((D_MODEL, 2 * D_FF)),
            _const_spec((3, D_FF)),
            _const_spec((1, D_FF)),
            _resident_spec((D_FF, D_MODEL)),
            _const_spec((1, D_MODEL)),
        ],
        out_specs=pl.BlockSpec((tm, D_MODEL), lambda i: (i, 0)),
        out_shape=jax.ShapeDtypeStruct((n, D_MODEL), F32),
        compiler_params=_cparams("parallel"),
        name="ffn",
    )(x, x, x, g, wu, cw, cb, wd, gf)


def _rope_tables(T):
    t = jnp.arange(T)

    def cos_sin(pos, half):
        freqs = ROPE_THETA ** (-jnp.arange(half, dtype=F32) / half)
        ang = pos.astype(F32)[:, None] * freqs[None, :]
        return jnp.cos(ang), jnp.sin(ang)

    def lanes(c, s):
        return jnp.tile(c, (1, 4)), jnp.concatenate([-jnp.tile(s, (1, 2)), jnp.tile(s, (1, 2))], axis=-1)

    lin = lanes(*cos_sin(t, HEAD_DIM // 2))
    cr, sr = cos_sin(t // GRID_W, HEAD_DIM // 4)
    cc, sc = cos_sin(t % GRID_W, HEAD_DIM // 4)
    ax = lanes(jnp.concatenate([cr, cc], -1), jnp.concatenate([sr, sc], -1))
    return jnp.concatenate([*lin, *ax], axis=-1)


def _na_bias_table(rpb):
    c = np.arange(GRID_W)
    qstart = np.clip(c - NA_COLS // 2, 0, GRID_W - NA_COLS)
    rel = c[None, :] - qstart[:, None]
    valid = (rel >= 0) & (rel < NA_COLS)
    dc = np.clip(c[None, :] - c[:, None], -(NA_COLS - 1), NA_COLS - 1) + (NA_COLS - 1)
    e = np.arange(NA_ROWS)
    m = np.arange(NA_ROWS)
    dr = m[None, :] - e[:, None] + (NA_ROWS - 1)
    onehot = np.zeros((2 * NA_COLS - 1, GRID_W * GRID_W), np.float32)
    onehot[dc.reshape(-1), np.arange(GRID_W * GRID_W)] = 1.0
    rows = rpb[:, dr].reshape(-1, 2 * NA_COLS - 1)
    tbl = jnp.dot(rows, onehot, precision=lax.Precision.HIGHEST)
    tbl = tbl.reshape(rpb.shape[0], NA_ROWS, NA_ROWS, GRID_W, GRID_W)
    tbl = jnp.where(valid[None, None, None], tbl, NEG)
    tbl = tbl.transpose(1, 0, 3, 2, 4)
    return tbl.reshape(NA_ROWS, 2, 2 * GRID_W, NA_ROWS * GRID_W).astype(F32)


def _rotary_order(width, axial):
    idx = np.arange(width)
    nb = width // LANES
    if axial:
        idx = np.moveaxis(idx.reshape(nb, 2, 2, 2, HEAD_DIM // 4), -2, -4)
    else:
        idx = np.swapaxes(idx.reshape(nb, 2, 2, HEAD_DIM // 2), -2, -3)
    return idx.reshape(width)


def _gqa_order():
    return np.concatenate([np.arange(HEAD_DIM) + HEAD_DIM * hd for hd in GQA_HEAD_ORDER])


def _in_column_order():
    gqa = _gqa_order()
    return np.concatenate([
        gqa[_rotary_order(256, True)], 256 + _rotary_order(128, True), np.arange(384, 512),
        512 + gqa[_rotary_order(256, False)], 768 + _rotary_order(128, False), np.arange(896, 1024),
        1024 + _rotary_order(512, False), np.arange(1536, IN_WIDTH)])


def _trunk(x, mem, p):
    B, T, _ = x.shape
    n = B * T
    x = x.reshape(n, D_MODEL)
    tab = _rope_tables(T)
    head_of_lane = (np.arange(LANES) % HEAD_DIM) // (HEAD_DIM // 2)
    bd = jnp.asarray(head_of_lane[:, None] == head_of_lane[None, :], BF16)
    mem2 = mem.reshape(B * MEM_LEN, D_MODEL)
    for l in range(DEPTH):
        qa, ka, va, qb, kb, vb, qc, kc, vc, qc4, kc4, vc4, qc16, kc16, vc16, qd, kd, vd = _proj(
            x, p["norm_mix_g"][l], p["w_in"][l], tab, p["gq"][l], p["gk"][l], bd, T)
        ya = _attn_a(qa, ka, va, B, T)
        yb = _band(qb, kb, vb, nseq=B, L=T, S=BAND_BLOCK, W=BAND_BLOCK + 2 * B_RADIUS, R=B_RADIUS,
                   halo=B_RADIUS, mode="sink", extra=p["sink"][l])
        cs = []
        for (window, d), qkv in zip(C_BRANCHES, ((qc, kc, vc), (qc4, kc4, vc4), (qc16, kc16, vc16))):
            r = window // (2 * d)
            blk = min(C_BLOCK, T // d)
            qs, ks, vs = (z.reshape(n, 256) for z in qkv)
            cs.append(_band(qs, ks, vs, nseq=B * d, L=T // d, S=blk, W=blk + 2 * r, R=r, halo=r, mode="lse"))
        yd = _band(qd, kd, vd, nseq=B, L=T, S=GRID_W, W=NA_ROWS * GRID_W, R=0, halo=(NA_ROWS // 2) * GRID_W,
                   mode="bias", extra=p["na_bias"][l])
        x, qx = _mixout(x, ya, yb, cs[0], cs[1], cs[2], yd, p["grp_g"][l], p["w_out"][l],
                        p["norm_x_g"][l], p["w_xq"][l], T)
        kv = _memkv(mem2, p["norm_mem_g"][l], p["w_xkv"][l])
        x = _cross(qx, kv, x, p["w_xo"][l], T)
        x = _ffn(x, p["norm_ffn_g"][l], p["w_up"][l], p["conv_w"][l], p["conv_b"][l], p["w_down"][l],
                 p["final_g"], T, final=(l == DEPTH - 1))
    return x.reshape(B, T, D_MODEL)


def _prepare(norm_mix_g, w_in, a_q_norm_g, a_k_norm_g, b_sink, d_rpb, grp_norm_g, w_out, norm_x_g,
             norm_mem_g, w_xq, w_xkv, w_xo, norm_ffn_g, w_up, conv_w, conv_b, w_down, final_norm_g):
    layers = range(norm_mix_g.shape[0])

    gqa = _gqa_order()
    in_cols = _in_column_order()
    out_rows = np.concatenate([gqa, GROUP_WIDTH + gqa, np.arange(2 * GROUP_WIDTH, D_MODEL)])
    qk_lanes = _rotary_order(LANES, axial=True)

    def grp_layer(g):
        return jnp.concatenate([g[0:2][:, gqa], g[2:4]], axis=0)

    return {
        "norm_mix_g": [norm_mix_g[l][None, :] for l in layers],
        "w_in": [w_in[l][:, in_cols].astype(BF16) for l in layers],
        "gq": [jnp.tile(a_q_norm_g[l], 2)[qk_lanes][None, :] * (HEAD_DIM ** -0.5 * LOG2E) for l in layers],
        "gk": [jnp.tile(a_k_norm_g[l], 2)[qk_lanes][None, :] for l in layers],
        "sink": [b_sink[l][np.array(GQA_HEAD_ORDER)].reshape(2, 2) * LOG2E for l in layers],
        "na_bias": [_na_bias_table(d_rpb[l] * LOG2E) for l in layers],
        "grp_g": [grp_layer(grp_norm_g[l]) for l in layers],
        "w_out": [w_out[l][out_rows].astype(BF16) for l in layers],
        "norm_x_g": [norm_x_g[l][None, :] for l in layers],
        "norm_mem_g": [norm_mem_g[l][None, :] for l in layers],
        "w_xq": [w_xq[l].astype(BF16) for l in layers],
        "w_xkv": [w_xkv[l].astype(BF16) for l in layers],
        "w_xo": [w_xo[l].astype(BF16) for l in layers],
        "norm_ffn_g": [norm_ffn_g[l][None, :] for l in layers],
        "w_up": [w_up[l].astype(BF16) for l in layers],
        "conv_w": [conv_w[l] for l in layers],
        "conv_b": [conv_b[l][None, :] for l in layers],
        "w_down": [w_down[l].astype(BF16) for l in layers],
        "final_g": final_norm_g[None, :],
    }


def kernel(x_prompt, x_sample, mem_prompt, mem_sample, norm_mix_g, w_in, a_q_norm_g, a_k_norm_g, b_sink,
           d_rpb, grp_norm_g, w_out, norm_x_g, norm_mem_g, w_xq, w_xkv, w_xo, norm_ffn_g, w_up, conv_w,
           conv_b, w_down, final_norm_g):
    p = _prepare(norm_mix_g, w_in, a_q_norm_g, a_k_norm_g, b_sink, d_rpb, grp_norm_g, w_out, norm_x_g,
                 norm_mem_g, w_xq, w_xkv, w_xo, norm_ffn_g, w_up, conv_w, conv_b, w_down, final_norm_g)
    return _trunk(x_prompt, mem_prompt, p), _trunk(x_sample, mem_sample, p)
```

```python
import functools
import math

import numpy as np
import jax
import jax.numpy as jnp
from jax import lax
from jax.experimental import pallas as pl
from jax.experimental.pallas import tpu as pltpu

F32 = jnp.float32
BF16 = jnp.bfloat16

D_MODEL = 1024
DEPTH = 4
HEAD_DIM = 64
GROUP_WIDTH = 256
GRID_W = 64
ROPE_THETA = 10000.0
B_RADIUS = 128
C_BRANCHES = ((128, 1), (512, 4), (2048, 16))
NA_ROWS = 8
NA_COLS = 16
MEM_LEN = 256
X_HEADS = 4
X_HEAD_DIM = D_MODEL // X_HEADS
D_FF = 2816
EPS = 1e-6
NEG = -1e30
IN_WIDTH = 2560

LANES = 128
VMEM_LIMIT = 56 * 1024 * 1024
TOKEN_TILE = 1024
FFN_TILE = 512
BAND_TOKENS = 2048
A_TQ = 512
A_TQ_ONE_BLOCK = 1024
A_TK = 8192
A_ROWS = 128
BAND_BLOCK = 256
C_BLOCK = 128
BAND_UNROLL_ROWS = BAND_TOKENS
LOG2E = 1.4426950408889634
MXU_COLS = 256
FF_CHUNKS = ((0, 6 * MXU_COLS), (6 * MXU_COLS, D_FF))

GQA_HEAD_ORDER = (0, 2, 1, 3)
NT_DIMS = (((1,), (1,)), ((), ()))


def _cparams(*sem):
    return pltpu.CompilerParams(dimension_semantics=sem, vmem_limit_bytes=VMEM_LIMIT)


def _const_spec(shape):
    n = len(shape)
    return pl.BlockSpec(shape, lambda *_: (0,) * n)


def _resident_spec(shape):
    n = len(shape)
    return pl.BlockSpec(shape, lambda *_: (0,) * n, pipeline_mode=pl.Buffered(1))


def _rms_rows(x, g):
    return x * lax.rsqrt(jnp.mean(x * x, axis=-1, keepdims=True) + EPS) * g


def _low_lanes():
    return lax.broadcasted_iota(jnp.int32, (1, LANES), 1) < HEAD_DIM


def _first_head_lanes(rotary):
    lane = lax.broadcasted_iota(jnp.int32, (1, LANES), 1)
    return (lane % HEAD_DIM) < HEAD_DIM // 2 if rotary else lane < HEAD_DIM


def _stack_heads(qb, low):
    zero = jnp.zeros_like(qb)
    return jnp.concatenate([jnp.where(low, qb, zero), jnp.where(low, zero, qb)], axis=0)


def _rope(y, c, s):
    return y * c + pltpu.roll(y, HEAD_DIM, 1) * s


def _proj_kernel(x_ref, g_ref, w_ref, tab_ref, gq_ref, gk_ref, bd_ref,
                 qa, ka, va, qb, kb, vb, qc, kc, vc, qc4, kc4, vc4, qc16, kc16, vc16, qd, kd, vd,
                 q_scr, k_scr, v_scr):
    tm = x_ref.shape[0]
    scale = HEAD_DIM ** -0.5 * LOG2E
    h = _rms_rows(x_ref[...], g_ref[...]).astype(BF16)

    def halves(y):
        return y[:, :LANES], y[:, LANES:]

    def tab(k):
        return tab_ref[:, k * LANES:(k + 1) * LANES]

    def rope_lin(y):
        return _rope(y, tab(0), tab(1))

    def rope_ax(y):
        return _rope(y, tab(2), tab(3))

    def qknorm(y, g):
        ss = jnp.dot((y * y).astype(BF16), bd_ref[...], preferred_element_type=F32) * (1.0 / HEAD_DIM)
        return y * lax.rsqrt(ss + EPS) * g

    def emit_c(scr, y0, y1, nat, subs):
        for hb, y in enumerate((y0, y1)):
            cols = slice(hb * LANES, (hb + 1) * LANES)
            scr[hb] = y
            nat[:, cols] = y.astype(BF16)
            for (_, d), ref in zip(C_BRANCHES[1:], subs):
                for r in range(d):
                    ref[0, r, :, cols] = scr[hb, pl.ds(r, tm // d, stride=d), :].astype(BF16)

    ys = [halves(jnp.dot(h, w_ref[:, c0:c0 + 256], preferred_element_type=F32))
          for c0 in range(0, IN_WIDTH, 256)]

    y0, y1 = ys[0]
    qa[:, :LANES] = rope_ax(qknorm(y0, gq_ref[...])).astype(BF16)
    qa[:, LANES:] = rope_ax(qknorm(y1, gq_ref[...])).astype(BF16)
    y0, y1 = ys[1]
    ka[...] = rope_ax(qknorm(y0, gk_ref[...])).astype(BF16)
    va[...] = y1.astype(BF16)

    y0, y1 = ys[2]
    qb[:, :LANES] = rope_lin(y0 * scale).astype(BF16)
    qb[:, LANES:] = rope_lin(y1 * scale).astype(BF16)
    y0, y1 = ys[3]
    kb[...] = rope_lin(y0).astype(BF16)
    vb[...] = y1.astype(BF16)

    y0, y1 = ys[4]
    emit_c(q_scr, rope_lin(y0 * scale), rope_lin(y1 * scale), qc, (qc4, qc16))
    y0, y1 = ys[5]
    emit_c(k_scr, rope_lin(y0), rope_lin(y1), kc, (kc4, kc16))
    y0, y1 = ys[6]
    emit_c(v_scr, y0, y1, vc, (vc4, vc16))

    y0, y1 = ys[7]
    qd[:, :LANES] = (y0 * scale).astype(BF16)
    qd[:, LANES:] = (y1 * scale).astype(BF16)
    for (y0, y1), ref in zip(ys[8:], (kd, vd)):
        ref[:, :LANES] = y0.astype(BF16)
        ref[:, LANES:] = y1.astype(BF16)


def _proj(x, g, w, tab, gq, gk, bd, T):
    n = x.shape[0]
    tm = TOKEN_TILE
    tpb = T // tm
    B = n // T
    row = lambda wd: (pl.BlockSpec((tm, wd), lambda i: (i, 0)), jax.ShapeDtypeStruct((n, wd), BF16))

    def slab(d):
        return (pl.BlockSpec((1, d, tm // d, 256), lambda i: (i // tpb, 0, i % tpb, 0)),
                jax.ShapeDtypeStruct((B, d, T // d, 256), BF16))

    outs = ([row(256), row(128), row(128)] * 2 + [row(256)] * 3
            + [slab(d) for _, d in C_BRANCHES[1:] for _ in range(3)] + [row(256)] * 3)
    return pl.pallas_call(
        _proj_kernel,
        grid=(n // tm,),
        in_specs=[
            pl.BlockSpec((tm, D_MODEL), lambda i: (i, 0)),
            _const_spec((1, D_MODEL)),
            _resident_spec((D_MODEL, IN_WIDTH)),
            pl.BlockSpec((tm, 4 * LANES), lambda i: (i % tpb, 0)),
            _const_spec((1, LANES)),
            _const_spec((1, LANES)),
            _const_spec((LANES, LANES)),
        ],
        out_specs=[o[0] for o in outs],
        out_shape=[o[1] for o in outs],
        scratch_shapes=[pltpu.VMEM((2, tm, LANES), F32)] * 3,
        compiler_params=_cparams("parallel"),
        name="proj",
    )(x, g, w, tab, gq, gk, bd)


def _attn_a_kernel(q_ref, k_ref, v_ref, o_ref, m_sc, acc_sc, *, nk):
    j = pl.program_id(2)
    tq = q_ref.shape[0]
    tk = k_ref.shape[0]

    @pl.when(j == 0)
    def _():
        m_sc[...] = jnp.full(m_sc.shape, NEG, F32)
        acc_sc[...] = jnp.zeros(acc_sc.shape, F32)

    low = _low_lanes()
    qfirst = _first_head_lanes(rotary=True)
    k = k_ref[...]
    v1 = jnp.concatenate([v_ref[...], jnp.ones((tk, LANES), BF16)], axis=1)
    for jb in range(2):
        q2 = _stack_heads(q_ref[:, jb * LANES:(jb + 1) * LANES], qfirst)
        for rb in range(2 * tq // A_ROWS):
            rows = slice(rb * A_ROWS, (rb + 1) * A_ROWS)
            s = lax.dot_general(q2[rows], k, NT_DIMS, preferred_element_type=F32)
            m = m_sc[jb, rows, :]
            m_new = jnp.maximum(m, jnp.max(s, axis=-1, keepdims=True))
            p = jnp.exp2(s - m_new).astype(BF16)
            acc_sc[jb, rows, :] = (acc_sc[jb, rows, :] * jnp.exp2(m - m_new)
                                   + jnp.dot(p, v1, preferred_element_type=F32))
            m_sc[jb, rows, :] = m_new

    @pl.when(j == nk - 1)
    def _():
        for jb in range(2):
            cols = slice(jb * LANES, (jb + 1) * LANES)
            acc = acc_sc[jb]
            num = jnp.where(low, acc[:tq, :LANES], acc[tq:, :LANES])
            den = jnp.where(low, acc[:tq, LANES:], acc[tq:, LANES:])
            o_ref[:, cols] = (num / den).astype(o_ref.dtype)


def _attn_a(q, k, v, B, T):
    n = q.shape[0]
    tk = min(A_TK, T)
    tq = A_TQ_ONE_BLOCK if tk == T else A_TQ
    nq, nk = T // tq, T // tk
    return pl.pallas_call(
        functools.partial(_attn_a_kernel, nk=nk),
        grid=(B, nq, nk),
        in_specs=[
            pl.BlockSpec((tq, 256), lambda b, i, j: (b * nq + i, 0)),
            pl.BlockSpec((tk, LANES), lambda b, i, j: (b * nk + j, 0)),
            pl.BlockSpec((tk, LANES), lambda b, i, j: (b * nk + j, 0)),
        ],
        out_specs=pl.BlockSpec((tq, 256), lambda b, i, j: (b * nq + i, 0)),
        out_shape=jax.ShapeDtypeStruct((n, 256), BF16),
        scratch_shapes=[
            pltpu.VMEM((2, 2 * tq, 1), F32),
            pltpu.VMEM((2, 2 * tq, 256), F32),
        ],
        compiler_params=_cparams("parallel", "parallel", "arbitrary"),
        name="attn_a",
    )(q, k, v)


def _band_kernel(*refs, S, W, halo, align, L, kvw, mode, unroll):
    if mode == "sink":
        sink_ref, q_ref, k_ref, v_ref, tab_ref, o_ref = refs
    elif mode == "lse":
        q_ref, k_ref, v_ref, tab_ref, o_ref, lse_ref = refs
    else:
        q_ref, k_ref, v_ref, tab_ref, o_ref = refs
    qi = pl.program_id(1)
    G, TQ = q_ref.shape[0], q_ref.shape[1]
    per = TQ // S
    low = _low_lanes()
    qfirst = _first_head_lanes(rotary=mode != "bias")
    ones = jnp.ones((W, LANES), BF16)
    first_half = lax.broadcasted_iota(jnp.int32, (2 * S, 1), 0) < S

    def body(it, carry):
        g = it // per
        i = it % per
        ig = qi * per + i
        start = pl.multiple_of(jnp.clip(ig * S - halo, 0, L - W), align)
        q0 = pl.multiple_of(i * S, S)
        qs = q_ref[g, pl.ds(q0, S), :]
        ks = k_ref[g, pl.ds(start, W), :]
        vs = v_ref[g, pl.ds(start, W), :]
        case = (ig * S - start) // (S if mode == "bias" else halo)
        for jb in range(2):
            cols = slice(jb * LANES, (jb + 1) * LANES)
            kcols = cols if kvw == 256 else slice(0, LANES)
            q2 = _stack_heads(qs[:, cols], qfirst)
            s = lax.dot_general(q2, ks[:, kcols], NT_DIMS, preferred_element_type=F32)
            s = s + (tab_ref[case, jb] if mode == "bias" else tab_ref[case])
            m = jnp.max(s, axis=-1, keepdims=True)
            if mode == "sink":
                sk = jnp.where(first_half, sink_ref[jb, 0], sink_ref[jb, 1])
                m = jnp.maximum(m, sk)
            p = jnp.exp2(s - m).astype(BF16)
            v1 = jnp.concatenate([vs[:, kcols], ones], axis=1)
            pv = jnp.dot(p, v1, preferred_element_type=F32)
            l = pv[:, LANES:]
            if mode == "sink":
                l = l + jnp.exp2(sk - m)
            num = jnp.where(low, pv[:S, :LANES], pv[S:, :LANES])
            den = jnp.where(low, l[:S], l[S:])
            o_ref[g, pl.ds(q0, S), cols] = (num / den).astype(o_ref.dtype)
            if mode == "lse":
                lse_ref[g, pl.ds(q0, S), cols] = (jnp.where(low, m[:S], m[S:]) + jnp.log2(den)) * (1.0 / LOG2E)
        return carry

    lax.fori_loop(0, G * per, body, 0, unroll=unroll)


def _band_mask_table(S, W, R, halo):
    rowq = np.arange(2 * S) % S
    col = np.arange(W)
    tabs = [np.where(np.abs(col[None, :] - rowq[:, None] - c * halo) <= R, 0.0, NEG) for c in range(3)]
    return jnp.asarray(np.stack(tabs), F32)


def _band(q, k, v, *, nseq, L, S, W, R, halo, mode, extra=None):
    kvw = k.shape[1]
    W = min(W, L)
    TQ = min(L, BAND_TOKENS)
    G = max(1, BAND_TOKENS // L)
    S = min(S, L)
    align = math.gcd(S, halo, L - W) if L > W else S
    unroll = max(1, BAND_UNROLL_ROWS // S)
    assert nseq % G == 0 and L % TQ == 0 and TQ % S == 0 and align % 16 == 0
    q3 = q.reshape(nseq, L, 256)
    k3 = k.reshape(nseq, L, kvw)
    v3 = v.reshape(nseq, L, kvw)
    qspec = pl.BlockSpec((G, TQ, 256), lambda s, i: (s, i, 0))
    kvspec = pl.BlockSpec((G, L, kvw), lambda s, i: (s, 0, 0))
    in_specs = [qspec, kvspec, kvspec]
    args = [q3, k3, v3]
    out_specs = qspec
    out_shape = jax.ShapeDtypeStruct((nseq, L, 256), BF16)
    tab = extra if mode == "bias" else _band_mask_table(S, W, R, halo)
    in_specs = in_specs + [_resident_spec(tab.shape)]
    args = args + [tab]
    if mode == "sink":
        in_specs = [pl.BlockSpec(memory_space=pltpu.SMEM)] + in_specs
        args = [extra] + args
    elif mode == "lse":
        out_specs = [qspec, qspec]
        out_shape = [out_shape, jax.ShapeDtypeStruct((nseq, L, 256), F32)]
    out = pl.pallas_call(
        functools.partial(_band_kernel, S=S, W=W, halo=halo, align=align, L=L, kvw=kvw, mode=mode,
                          unroll=unroll),
        grid=(nseq // G, L // TQ),
        in_specs=in_specs,
        out_specs=out_specs,
        out_shape=out_shape,
        compiler_params=_cparams("parallel", "arbitrary"),
        name="band_" + mode,
    )(*args)
    if mode == "lse":
        return out[0].reshape(nseq * L, 256), out[1].reshape(nseq * L, 256)
    return out.reshape(nseq * L, 256)


def _mixout_kernel(x_ref, ya_ref, yb_ref, c1o, c1l, c2o, c2l, c3o, c3l, yd_ref,
                   gg_ref, wo_ref, gx_ref, wq_ref, xo_ref, qx_ref, *scr):
    tm = x_ref.shape[0]

    def gn(y, gi):
        return _rms_rows(y.astype(F32), gg_ref[gi:gi + 1, :]).astype(BF16)

    def token_order(ref, buf):
        d = ref.shape[1]
        for hb in range(2):
            for r in range(d):
                buf[hb, pl.ds(r, tm // d, stride=d), :] = ref[0, r, :, hb * LANES:(hb + 1) * LANES].astype(F32)
        return jnp.concatenate([buf[0], buf[1]], axis=-1)

    o1, l1 = c1o[...].astype(F32), c1l[...]
    o2, l2 = token_order(c2o, scr[0]), token_order(c2l, scr[1])
    o3, l3 = token_order(c3o, scr[2]), token_order(c3l, scr[3])
    mx = jnp.maximum(jnp.maximum(l1, l2), l3)
    e1, e2, e3 = jnp.exp(l1 - mx), jnp.exp(l2 - mx), jnp.exp(l3 - mx)
    yc = (e1 * o1 + e2 * o2 + e3 * o3) / (e1 + e2 + e3)
    ycat = jnp.concatenate([gn(ya_ref[...], 0), gn(yb_ref[...], 1), gn(yc, 2), gn(yd_ref[...], 3)], axis=-1)
    xn = x_ref[...] + jnp.dot(ycat, wo_ref[...], preferred_element_type=F32)
    xo_ref[...] = xn
    hq = _rms_rows(xn, gx_ref[...]).astype(BF16)
    qx_ref[...] = (jnp.dot(hq, wq_ref[...], preferred_element_type=F32) * (X_HEAD_DIM ** -0.5)).astype(BF16)


def _mixout(x, ya, yb, c1, c2, c3, yd, gg, wo, gx, wq, T):
    n = x.shape[0]
    B = n // T
    tm = TOKEN_TILE
    tpb = T // tm
    row = lambda wd: pl.BlockSpec((tm, wd), lambda i: (i, 0))
    slab = lambda d: pl.BlockSpec((1, d, tm // d, 256), lambda i: (i // tpb, 0, i % tpb, 0))
    d2, d3 = C_BRANCHES[1][1], C_BRANCHES[2][1]
    return pl.pallas_call(
        _mixout_kernel,
        grid=(n // tm,),
        in_specs=[row(D_MODEL)] + [row(256)] * 4 + [slab(d2)] * 2 + [slab(d3)] * 2 + [row(256)] + [
            _const_spec((4, GROUP_WIDTH)),
            _resident_spec((D_MODEL, D_MODEL)),
            _const_spec((1, D_MODEL)),
            _resident_spec((D_MODEL, D_MODEL)),
        ],
        out_specs=[row(D_MODEL), row(D_MODEL)],
        out_shape=[jax.ShapeDtypeStruct((n, D_MODEL), F32), jax.ShapeDtypeStruct((n, D_MODEL), BF16)],
        scratch_shapes=[pltpu.VMEM((2, tm, LANES), F32)] * 4,
        compiler_params=_cparams("parallel"),
        name="mixout",
    )(x, ya, yb, c1[0], c1[1], *[z.reshape(B, d2, T // d2, 256) for z in c2],
      *[z.reshape(B, d3, T // d3, 256) for z in c3], yd, gg, wo, gx, wq)


def _memkv_kernel(m_ref, g_ref, w_ref, kv_ref):
    h = _rms_rows(m_ref[...], g_ref[...]).astype(BF16)
    kv_ref[...] = jnp.dot(h, w_ref[...], preferred_element_type=F32).astype(BF16)


def _memkv(mem, g, w):
    n = mem.shape[0]
    return pl.pallas_call(
        _memkv_kernel,
        grid=(n // MEM_LEN,),
        in_specs=[
            pl.BlockSpec((MEM_LEN, D_MODEL), lambda i: (i, 0)),
            _const_spec((1, D_MODEL)),
            _resident_spec((D_MODEL, 2 * D_MODEL)),
        ],
        out_specs=pl.BlockSpec((MEM_LEN, 2 * D_MODEL), lambda i: (i, 0)),
        out_shape=jax.ShapeDtypeStruct((n, 2 * D_MODEL), BF16),
        compiler_params=_cparams("parallel"),
        name="memkv",
    )(mem, g, w)


def _cross_kernel(qx_ref, kv_ref, x_ref, wo_ref, xo_ref):
    outs = []
    for h in range(X_HEADS):
        cols = slice(h * X_HEAD_DIM, (h + 1) * X_HEAD_DIM)
        vcols = slice(D_MODEL + h * X_HEAD_DIM, D_MODEL + (h + 1) * X_HEAD_DIM)
        s = lax.dot_general(qx_ref[:, cols], kv_ref[:, cols], NT_DIMS, preferred_element_type=F32)
        m = jnp.max(s, axis=-1, keepdims=True)
        p = jnp.exp(s - m)
        l = jnp.sum(p, axis=-1, keepdims=True)
        o = jnp.dot(p.astype(BF16), kv_ref[:, vcols], preferred_element_type=F32) / l
        outs.append(o.astype(BF16))
    o = jnp.concatenate(outs, axis=-1)
    xo_ref[...] = x_ref[...] + jnp.dot(o, wo_ref[...], preferred_element_type=F32)


def _cross(qx, kv, x, wo, T):
    n = x.shape[0]
    tm = TOKEN_TILE
    tpb = T // tm
    row = lambda wd: pl.BlockSpec((tm, wd), lambda i: (i, 0))
    return pl.pallas_call(
        _cross_kernel,
        grid=(n // tm,),
        in_specs=[
            row(D_MODEL),
            pl.BlockSpec((MEM_LEN, 2 * D_MODEL), lambda i: (i // tpb, 0)),
            row(D_MODEL),
            _resident_spec((D_MODEL, D_MODEL)),
        ],
        out_specs=row(D_MODEL),
        out_shape=jax.ShapeDtypeStruct((n, D_MODEL), F32),
        compiler_params=_cparams("parallel"),
        name="cross",
    )(qx, kv, x, wo)


HALO = 16


def _ffn_kernel(xp_ref, x_ref, xn_ref, g_ref, wu_ref, cw_ref, cb_ref, wd_ref, gf_ref, o_ref, *, tpb, final):
    i = pl.program_id(0)
    tm = x_ref.shape[0]
    x = x_ref[...]
    xe = jnp.concatenate([xp_ref[...], x, xn_ref[...]], axis=0)
    he = _rms_rows(xe, g_ref[...]).astype(BF16)
    hc = he[HALO:HALO + tm]
    rows = lax.broadcasted_iota(jnp.int32, (tm + 2 * HALO, 1), 0)
    seq_first = (i % tpb) == 0
    seq_last = (i % tpb) == tpb - 1
    keep = jnp.logical_and(jnp.logical_or(rows >= HALO, jnp.logical_not(seq_first)),
                           jnp.logical_or(rows < HALO + tm, jnp.logical_not(seq_last)))
    keep = keep.astype(F32)
    acc = x
    for c0, c1 in FF_CHUNKS:
        gc = slice(c0, c1)
        uc = slice(D_FF + c0, D_FF + c1)
        ge = jnp.dot(he, wu_ref[:, gc], preferred_element_type=F32) * keep
        val = jnp.dot(hc, wu_ref[:, uc], preferred_element_type=F32)
        gate = (ge[HALO - 1:HALO - 1 + tm] * cw_ref[0:1, gc] + ge[HALO:HALO + tm] * cw_ref[1:2, gc]
                + ge[HALO + 1:HALO + 1 + tm] * cw_ref[2:3, gc] + cb_ref[:, gc])
        act = 0.5 * gate * (1.0 + lax.erf(gate * (2.0 ** -0.5))) * val
        acc = acc + jnp.dot(act.astype(BF16), wd_ref[gc, :], preferred_element_type=F32)
    if final:
        acc = _rms_rows(acc, gf_ref[...])
    o_ref[...] = acc


def _ffn(x, g, wu, cw, cb, wd, gf, T, final):
    n = x.shape[0]
    tm = FFN_TILE
    tpb = T // tm
    hb = tm // HALO
    nhb = n // HALO
    return pl.pallas_call(
        functools.partial(_ffn_kernel, tpb=tpb, final=final),
        grid=(n // tm,),
        in_specs=[
            pl.BlockSpec((HALO, D_MODEL), lambda i: (jnp.maximum(i * hb - 1, 0), 0)),
            pl.BlockSpec((tm, D_MODEL), lambda i: (i, 0)),
            pl.BlockSpec((HALO, D_MODEL), lambda i: (jnp.minimum((i + 1) * hb, nhb - 1), 0)),
            _const_spec((1, D_MODEL)),
            _resident_spec((D_MODEL, 2 * D_FF)),
            _const_spec((3, D_FF)),
            _const_spec((1, D_FF)),
            _resident_spec((D_FF, D_MODEL)),
            _const_spec((1, D_MODEL)),
        ],
        out_specs=pl.BlockSpec((tm, D_MODEL), lambda i: (i, 0)),
        out_shape=jax.ShapeDtypeStruct((n, D_MODEL), F32),
        compiler_params=_cparams("parallel"),
        name="ffn",
    )(x, x, x, g, wu, cw, cb, wd, gf)


def _rope_tables(T):
    t = jnp.arange(T)

    def cos_sin(pos, half):
        freqs = ROPE_THETA ** (-jnp.arange(half, dtype=F32) / half)
        ang = pos.astype(F32)[:, None] * freqs[None, :]
        return jnp.cos(ang), jnp.sin(ang)

    def lanes(c, s):
        return jnp.tile(c, (1, 4)), jnp.concatenate([-jnp.tile(s, (1, 2)), jnp.tile(s, (1, 2))], axis=-1)

    lin = lanes(*cos_sin(t, HEAD_DIM // 2))
    cr, sr = cos_sin(t // GRID_W, HEAD_DIM // 4)
    cc, sc = cos_sin(t % GRID_W, HEAD_DIM // 4)
    ax = lanes(jnp.concatenate([cr, cc], -1), jnp.concatenate([sr, sc], -1))
    return jnp.concatenate([*lin, *ax], axis=-1)


def _na_bias_table(rpb):
    c = np.arange(GRID_W)
    qstart = np.clip(c - NA_COLS // 2, 0, GRID_W - NA_COLS)
    rel = c[None, :] - qstart[:, None]
    valid = (rel >= 0) & (rel < NA_COLS)
    dc = np.clip(c[None, :] - c[:, None], -(NA_COLS - 1), NA_COLS - 1) + (NA_COLS - 1)
    e = np.arange(NA_ROWS)
    m = np.arange(NA_ROWS)
    dr = m[None, :] - e[:, None] + (NA_ROWS - 1)
    onehot = np.zeros((2 * NA_COLS - 1, GRID_W * GRID_W), np.float32)
    onehot[dc.reshape(-1), np.arange(GRID_W * GRID_W)] = 1.0
    rows = rpb[:, dr].reshape(-1, 2 * NA_COLS - 1)
    tbl = jnp.dot(rows, onehot, precision=lax.Precision.HIGHEST)
    tbl = tbl.reshape(rpb.shape[0], NA_ROWS, NA_ROWS, GRID_W, GRID_W)
    tbl = jnp.where(valid[None, None, None], tbl, NEG)
    tbl = tbl.transpose(1, 0, 3, 2, 4)
    return tbl.reshape(NA_ROWS, 2, 2 * GRID_W, NA_ROWS * GRID_W).astype(F32)


def _rotary_order(width, axial):
    idx = np.arange(width)
    nb = width // LANES
    if axial:
        idx = np.moveaxis(idx.reshape(nb, 2, 2, 2, HEAD_DIM // 4), -2, -4)
    else:
        idx = np.swapaxes(idx.reshape(nb, 2, 2, HEAD_DIM // 2), -2, -3)
    return idx.reshape(width)


def _gqa_order():
    return np.concatenate([np.arange(HEAD_DIM) + HEAD_DIM * hd for hd in GQA_HEAD_ORDER])


def _in_column_order():
    gqa = _gqa_order()
    return np.concatenate([
        gqa[_rotary_order(256, True)], 256 + _rotary_order(128, True), np.arange(384, 512),
        512 + gqa[_rotary_order(256, False)], 768 + _rotary_order(128, False), np.arange(896, 1024),
        1024 + _rotary_order(512, False), np.arange(1536, IN_WIDTH)])


def _trunk(x, mem, p):
    B, T, _ = x.shape
    n = B * T
    x = x.reshape(n, D_MODEL)
    tab = _rope_tables(T)
    head_of_lane = (np.arange(LANES) % HEAD_DIM) // (HEAD_DIM // 2)
    bd = jnp.asarray(head_of_lane[:, None] == head_of_lane[None, :], BF16)
    mem2 = mem.reshape(B * MEM_LEN, D_MODEL)
    for l in range(DEPTH):
        qa, ka, va, qb, kb, vb, qc, kc, vc, qc4, kc4, vc4, qc16, kc16, vc16, qd, kd, vd = _proj(
            x, p["norm_mix_g"][l], p["w_in"][l], tab, p["gq"][l], p["gk"][l], bd, T)
        ya = _attn_a(qa, ka, va, B, T)
        yb = _band(qb, kb, vb, nseq=B, L=T, S=BAND_BLOCK, W=BAND_BLOCK + 2 * B_RADIUS, R=B_RADIUS,
                   halo=B_RADIUS, mode="sink", extra=p["sink"][l])
        cs = []
        for (window, d), qkv in zip(C_BRANCHES, ((qc, kc, vc), (qc4, kc4, vc4), (qc16, kc16, vc16))):
            r = window // (2 * d)
            blk = min(C_BLOCK, T // d)
            qs, ks, vs = (z.reshape(n, 256) for z in qkv)
            cs.append(_band(qs, ks, vs, nseq=B * d, L=T // d, S=blk, W=blk + 2 * r, R=r, halo=r, mode="lse"))
        yd = _band(qd, kd, vd, nseq=B, L=T, S=GRID_W, W=NA_ROWS * GRID_W, R=0, halo=(NA_ROWS // 2) * GRID_W,
                   mode="bias", extra=p["na_bias"][l])
        x, qx = _mixout(x, ya, yb, cs[0], cs[1], cs[2], yd, p["grp_g"][l], p["w_out"][l],
                        p["norm_x_g"][l], p["w_xq"][l], T)
        kv = _memkv(mem2, p["norm_mem_g"][l], p["w_xkv"][l])
        x = _cross(qx, kv, x, p["w_xo"][l], T)
        x = _ffn(x, p["norm_ffn_g"][l], p["w_up"][l], p["conv_w"][l], p["conv_b"][l], p["w_down"][l],
                 p["final_g"], T, final=(l == DEPTH - 1))
    return x.reshape(B, T, D_MODEL)


def _prepare(norm_mix_g, w_in, a_q_norm_g, a_k_norm_g, b_sink, d_rpb, grp_norm_g, w_out, norm_x_g,
             norm_mem_g, w_xq, w_xkv, w_xo, norm_ffn_g, w_up, conv_w, conv_b, w_down, final_norm_g):
    layers = range(norm_mix_g.shape[0])

    gqa = _gqa_order()
    in_cols = _in_column_order()
    out_rows = np.concatenate([gqa, GROUP_WIDTH + gqa, np.arange(2 * GROUP_WIDTH, D_MODEL)])
    qk_lanes = _rotary_order(LANES, axial=True)

    def grp_layer(g):
        return jnp.concatenate([g[0:2][:, gqa], g[2:4]], axis=0)

    return {
        "norm_mix_g": [norm_mix_g[l][None, :] for l in layers],
        "w_in": [w_in[l][:, in_cols].astype(BF16) for l in layers],
        "gq": [jnp.tile(a_q_norm_g[l], 2)[qk_lanes][None, :] * (HEAD_DIM ** -0.5 * LOG2E) for l in layers],
        "gk": [jnp.tile(a_k_norm_g[l], 2)[qk_lanes][None, :] for l in layers],
        "sink": [b_sink[l][np.array(GQA_HEAD_ORDER)].reshape(2, 2) * LOG2E for l in layers],
        "na_bias": [_na_bias_table(d_rpb[l] * LOG2E) for l in layers],
        "grp_g": [grp_layer(grp_norm_g[l]) for l in layers],
        "w_out": [w_out[l][out_rows].astype(BF16) for l in layers],
        "norm_x_g": [norm_x_g[l][None, :] for l in layers],
        "norm_mem_g": [norm_mem_g[l][None, :] for l in layers],
        "w_xq": [w_xq[l].astype(BF16) for l in layers],
        "w_xkv": [w_xkv[l].astype(BF16) for l in layers],
        "w_xo": [w_xo[l].astype(BF16) for l in layers],
        "norm_ffn_g": [norm_ffn_g[l][None, :] for l in layers],
        "w_up": [w_up[l].astype(BF16) for l in layers],
        "conv_w": [conv_w[l] for l in layers],
        "conv_b": [conv_b[l][None, :] for l in layers],
        "w_down": [w_down[l].astype(BF16) for l in layers],
        "final_g": final_norm_g[None, :],
    }


def kernel(x_prompt, x_sample, mem_prompt, mem_sample, norm_mix_g, w_in, a_q_norm_g, a_k_norm_g, b_sink,
           d_rpb, grp_norm_g, w_out, norm_x_g, norm_mem_g, w_xq, w_xkv, w_xo, norm_ffn_g, w_up, conv_w,
           conv_b, w_down, final_norm_g):
    p = _prepare(norm_mix_g, w_in, a_q_norm_g, a_k_norm_g, b_sink, d_rpb, grp_norm_g, w_out, norm_x_g,
                 norm_mem_g, w_xq, w_xkv, w_xo, norm_ffn_g, w_up, conv_w, conv_b, w_down, final_norm_g)
    return _trunk(x_prompt, mem_prompt, p), _trunk(x_sample, mem_sample, p)
```

```python
import functools
import math

import numpy as np
import jax
import jax.numpy as jnp
from jax import lax
from jax.experimental import pallas as pl
from jax.experimental.pallas import tpu as pltpu

F32 = jnp.float32
BF16 = jnp.bfloat16

D_MODEL = 1024
DEPTH = 4
HEAD_DIM = 64
GROUP_WIDTH = 256
GRID_W = 64
ROPE_THETA = 10000.0
B_RADIUS = 128
C_BRANCHES = ((128, 1), (512, 4), (2048, 16))
NA_ROWS = 8
NA_COLS = 16
MEM_LEN = 256
X_HEADS = 4
X_HEAD_DIM = D_MODEL // X_HEADS
D_FF = 2816
EPS = 1e-6
NEG = -1e30
IN_WIDTH = 2560

LANES = 128
VMEM_LIMIT = 56 * 1024 * 1024
TOKEN_TILE = 1024
FFN_TILE = 512
BAND_TOKENS = 4096
A_TQ = 512
A_TQ_ONE_BLOCK = 1024
A_TK = 8192
A_ROWS = 128
BAND_BLOCK = 256
C_BLOCK = 128
BAND_UNROLL_ROWS = BAND_TOKENS
LOG2E = 1.4426950408889634
MXU_COLS = 256
FF_CHUNKS = ((0, 6 * MXU_COLS), (6 * MXU_COLS, D_FF))

GQA_HEAD_ORDER = (0, 2, 1, 3)
NT_DIMS = (((1,), (1,)), ((), ()))


def _cparams(*sem):
    return pltpu.CompilerParams(dimension_semantics=sem, vmem_limit_bytes=VMEM_LIMIT)


def _const_spec(shape):
    n = len(shape)
    return pl.BlockSpec(shape, lambda *_: (0,) * n)


def _resident_spec(shape):
    n = len(shape)
    return pl.BlockSpec(shape, lambda *_: (0,) * n, pipeline_mode=pl.Buffered(1))


def _rms_rows(x, g):
    return x * lax.rsqrt(jnp.mean(x * x, axis=-1, keepdims=True) + EPS) * g


def _low_lanes():
    return lax.broadcasted_iota(jnp.int32, (1, LANES), 1) < HEAD_DIM


def _first_head_lanes(rotary):
    lane = lax.broadcasted_iota(jnp.int32, (1, LANES), 1)
    return (lane % HEAD_DIM) < HEAD_DIM // 2 if rotary else lane < HEAD_DIM


def _stack_heads(qb, low):
    zero = jnp.zeros_like(qb)
    return jnp.concatenate([jnp.where(low, qb, zero), jnp.where(low, zero, qb)], axis=0)


def _rope(y, c, s):
    return y * c + pltpu.roll(y, HEAD_DIM, 1) * s


def _proj_kernel(x_ref, g_ref, w_ref, tab_ref, gq_ref, gk_ref, bd_ref,
                 qa, ka, va, qb, kb, vb, qc, kc, vc, qc4, kc4, vc4, qc16, kc16, vc16, qd, kd, vd,
                 q_scr, k_scr, v_scr):
    tm = x_ref.shape[0]
    scale = HEAD_DIM ** -0.5 * LOG2E
    h = _rms_rows(x_ref[...], g_ref[...]).astype(BF16)

    def halves(y):
        return y[:, :LANES], y[:, LANES:]

    def tab(k):
        return tab_ref[:, k * LANES:(k + 1) * LANES]

    def rope_lin(y):
        return _rope(y, tab(0), tab(1))

    def rope_ax(y):
        return _rope(y, tab(2), tab(3))

    def qknorm(y, g):
        ss = jnp.dot((y * y).astype(BF16), bd_ref[...], preferred_element_type=F32) * (1.0 / HEAD_DIM)
        return y * lax.rsqrt(ss + EPS) * g

    def emit_c(scr, y0, y1, nat, subs):
        for hb, y in enumerate((y0, y1)):
            cols = slice(hb * LANES, (hb + 1) * LANES)
            scr[hb] = y
            nat[:, cols] = y.astype(BF16)
            for (_, d), ref in zip(C_BRANCHES[1:], subs):
                for r in range(d):
                    ref[0, r, :, cols] = scr[hb, pl.ds(r, tm // d, stride=d), :].astype(BF16)

    ys = [halves(jnp.dot(h, w_ref[:, c0:c0 + 256], preferred_element_type=F32))
          for c0 in range(0, IN_WIDTH, 256)]

    y0, y1 = ys[0]
    qa[:, :LANES] = rope_ax(qknorm(y0, gq_ref[...])).astype(BF16)
    qa[:, LANES:] = rope_ax(qknorm(y1, gq_ref[...])).astype(BF16)
    y0, y1 = ys[1]
    ka[...] = rope_ax(qknorm(y0, gk_ref[...])).astype(BF16)
    va[...] = y1.astype(BF16)

    y0, y1 = ys[2]
    qb[:, :LANES] = rope_lin(y0 * scale).astype(BF16)
    qb[:, LANES:] = rope_lin(y1 * scale).astype(BF16)
    y0, y1 = ys[3]
    kb[...] = rope_lin(y0).astype(BF16)
    vb[...] = y1.astype(BF16)

    y0, y1 = ys[4]
    emit_c(q_scr, rope_lin(y0 * scale), rope_lin(y1 * scale), qc, (qc4, qc16))
    y0, y1 = ys[5]
    emit_c(k_scr, rope_lin(y0), rope_lin(y1), kc, (kc4, kc16))
    y0, y1 = ys[6]
    emit_c(v_scr, y0, y1, vc, (vc4, vc16))

    y0, y1 = ys[7]
    qd[:, :LANES] = (y0 * scale).astype(BF16)
    qd[:, LANES:] = (y1 * scale).astype(BF16)
    for (y0, y1), ref in zip(ys[8:], (kd, vd)):
        ref[:, :LANES] = y0.astype(BF16)
        ref[:, LANES:] = y1.astype(BF16)


def _proj(x, g, w, tab, gq, gk, bd, T):
    n = x.shape[0]
    tm = TOKEN_TILE
    tpb = T // tm
    B = n // T
    row = lambda wd: (pl.BlockSpec((tm, wd), lambda i: (i, 0)), jax.ShapeDtypeStruct((n, wd), BF16))

    def slab(d):
        return (pl.BlockSpec((1, d, tm // d, 256), lambda i: (i // tpb, 0, i % tpb, 0)),
                jax.ShapeDtypeStruct((B, d, T // d, 256), BF16))

    outs = ([row(256), row(128), row(128)] * 2 + [row(256)] * 3
            + [slab(d) for _, d in C_BRANCHES[1:] for _ in range(3)] + [row(256)] * 3)
    return pl.pallas_call(
        _proj_kernel,
        grid=(n // tm,),
        in_specs=[
            pl.BlockSpec((tm, D_MODEL), lambda i: (i, 0)),
            _const_spec((1, D_MODEL)),
            _resident_spec((D_MODEL, IN_WIDTH)),
            pl.BlockSpec((tm, 4 * LANES), lambda i: (i % tpb, 0)),
            _const_spec((1, LANES)),
            _const_spec((1, LANES)),
            _const_spec((LANES, LANES)),
        ],
        out_specs=[o[0] for o in outs],
        out_shape=[o[1] for o in outs],
        scratch_shapes=[pltpu.VMEM((2, tm, LANES), F32)] * 3,
        compiler_params=_cparams("parallel"),
        name="proj",
    )(x, g, w, tab, gq, gk, bd)


def _attn_a_kernel(q_ref, k_ref, v_ref, o_ref, m_sc, acc_sc, *, nk):
    j = pl.program_id(2)
    tq = q_ref.shape[0]
    tk = k_ref.shape[0]

    @pl.when(j == 0)
    def _():
        m_sc[...] = jnp.full(m_sc.shape, NEG, F32)
        acc_sc[...] = jnp.zeros(acc_sc.shape, F32)

    low = _low_lanes()
    qfirst = _first_head_lanes(rotary=True)
    k = k_ref[...]
    v1 = jnp.concatenate([v_ref[...], jnp.ones((tk, LANES), BF16)], axis=1)
    for jb in range(2):
        q2 = _stack_heads(q_ref[:, jb * LANES:(jb + 1) * LANES], qfirst)
        for rb in range(2 * tq // A_ROWS):
            rows = slice(rb * A_ROWS, (rb + 1) * A_ROWS)
            s = lax.dot_general(q2[rows], k, NT_DIMS, preferred_element_type=F32)
            m = m_sc[jb, rows, :]
            m_new = jnp.maximum(m, jnp.max(s, axis=-1, keepdims=True))
            p = jnp.exp2(s - m_new).astype(BF16)
            acc_sc[jb, rows, :] = (acc_sc[jb, rows, :] * jnp.exp2(m - m_new)
                                   + jnp.dot(p, v1, preferred_element_type=F32))
            m_sc[jb, rows, :] = m_new

    @pl.when(j == nk - 1)
    def _():
        for jb in range(2):
            cols = slice(jb * LANES, (jb + 1) * LANES)
            acc = acc_sc[jb]
            num = jnp.where(low, acc[:tq, :LANES], acc[tq:, :LANES])
            den = jnp.where(low, acc[:tq, LANES:], acc[tq:, LANES:])
            o_ref[:, cols] = (num / den).astype(o_ref.dtype)


def _attn_a(q, k, v, B, T):
    n = q.shape[0]
    tk = min(A_TK, T)
    tq = A_TQ_ONE_BLOCK if tk == T else A_TQ
    nq, nk = T // tq, T // tk
    return pl.pallas_call(
        functools.partial(_attn_a_kernel, nk=nk),
        grid=(B, nq, nk),
        in_specs=[
            pl.BlockSpec((tq, 256), lambda b, i, j: (b * nq + i, 0)),
            pl.BlockSpec((tk, LANES), lambda b, i, j: (b * nk + j, 0)),
            pl.BlockSpec((tk, LANES), lambda b, i, j: (b * nk + j, 0)),
        ],
        out_specs=pl.BlockSpec((tq, 256), lambda b, i, j: (b * nq + i, 0)),
        out_shape=jax.ShapeDtypeStruct((n, 256), BF16),
        scratch_shapes=[
            pltpu.VMEM((2, 2 * tq, 1), F32),
            pltpu.VMEM((2, 2 * tq, 256), F32),
        ],
        compiler_params=_cparams("parallel", "parallel", "arbitrary"),
        name="attn_a",
    )(q, k, v)


def _band_kernel(*refs, S, W, halo, align, L, kvw, mode, unroll):
    if mode == "sink":
        sink_ref, q_ref, k_ref, v_ref, tab_ref, o_ref = refs
    elif mode == "lse":
        q_ref, k_ref, v_ref, tab_ref, o_ref, lse_ref = refs
    else:
        q_ref, k_ref, v_ref, tab_ref, o_ref = refs
    qi = pl.program_id(1)
    G, TQ = q_ref.shape[0], q_ref.shape[1]
    per = TQ // S
    low = _low_lanes()
    qfirst = _first_head_lanes(rotary=mode != "bias")
    ones = jnp.ones((W, LANES), BF16)
    first_half = lax.broadcasted_iota(jnp.int32, (2 * S, 1), 0) < S

    def body(it, carry):
        g = it // per
        i = it % per
        ig = qi * per + i
        start = pl.multiple_of(jnp.clip(ig * S - halo, 0, L - W), align)
        q0 = pl.multiple_of(i * S, S)
        qs = q_ref[g, pl.ds(q0, S), :]
        ks = k_ref[g, pl.ds(start, W), :]
        vs = v_ref[g, pl.ds(start, W), :]
        case = (ig * S - start) // (S if mode == "bias" else halo)
        for jb in range(2):
            cols = slice(jb * LANES, (jb + 1) * LANES)
            kcols = cols if kvw == 256 else slice(0, LANES)
            q2 = _stack_heads(qs[:, cols], qfirst)
            s = lax.dot_general(q2, ks[:, kcols], NT_DIMS, preferred_element_type=F32)
            s = s + (tab_ref[case, jb] if mode == "bias" else tab_ref[case])
            m = jnp.max(s, axis=-1, keepdims=True)
            if mode == "sink":
                sk = jnp.where(first_half, sink_ref[jb, 0], sink_ref[jb, 1])
                m = jnp.maximum(m, sk)
            p = jnp.exp2(s - m).astype(BF16)
            v1 = jnp.concatenate([vs[:, kcols], ones], axis=1)
            pv = jnp.dot(p, v1, preferred_element_type=F32)
            l = pv[:, LANES:]
            if mode == "sink":
                l = l + jnp.exp2(sk - m)
            num = jnp.where(low, pv[:S, :LANES], pv[S:, :LANES])
            den = jnp.where(low, l[:S], l[S:])
            o_ref[g, pl.ds(q0, S), cols] = (num / den).astype(o_ref.dtype)
            if mode == "lse":
                lse_ref[g, pl.ds(q0, S), cols] = (jnp.where(low, m[:S], m[S:]) + jnp.log2(den)) * (1.0 / LOG2E)
        return carry

    lax.fori_loop(0, G * per, body, 0, unroll=unroll)


def _band_mask_table(S, W, R, halo):
    rowq = np.arange(2 * S) % S
    col = np.arange(W)
    tabs = [np.where(np.abs(col[None, :] - rowq[:, None] - c * halo) <= R, 0.0, NEG) for c in range(3)]
    return jnp.asarray(np.stack(tabs), F32)


def _band(q, k, v, *, nseq, L, S, W, R, halo, mode, extra=None):
    kvw = k.shape[1]
    W = min(W, L)
    TQ = min(L, BAND_TOKENS)
    G = max(1, BAND_TOKENS // L)
    S = min(S, L)
    align = math.gcd(S, halo, L - W) if L > W else S
    unroll = max(1, BAND_UNROLL_ROWS // S)
    assert nseq % G == 0 and L % TQ == 0 and TQ % S == 0 and align % 16 == 0
    q3 = q.reshape(nseq, L, 256)
    k3 = k.reshape(nseq, L, kvw)
    v3 = v.reshape(nseq, L, kvw)
    qspec = pl.BlockSpec((G, TQ, 256), lambda s, i: (s, i, 0))
    kvspec = pl.BlockSpec((G, L, kvw), lambda s, i: (s, 0, 0))
    in_specs = [qspec, kvspec, kvspec]
    args = [q3, k3, v3]
    out_specs = qspec
    out_shape = jax.ShapeDtypeStruct((nseq, L, 256), BF16)
    tab = extra if mode == "bias" else _band_mask_table(S, W, R, halo)
    in_specs = in_specs + [_resident_spec(tab.shape)]
    args = args + [tab]
    if mode == "sink":
        in_specs = [pl.BlockSpec(memory_space=pltpu.SMEM)] + in_specs
        args = [extra] + args
    elif mode == "lse":
        out_specs = [qspec, qspec]
        out_shape = [out_shape, jax.ShapeDtypeStruct((nseq, L, 256), F32)]
    out = pl.pallas_call(
        functools.partial(_band_kernel, S=S, W=W, halo=halo, align=align, L=L, kvw=kvw, mode=mode,
                          unroll=unroll),
        grid=(nseq // G, L // TQ),
        in_specs=in_specs,
        out_specs=out_specs,
        out_shape=out_shape,
        compiler_params=_cparams("parallel", "arbitrary"),
        name="band_" + mode,
    )(*args)
    if mode == "lse":
        return out[0].reshape(nseq * L, 256), out[1].reshape(nseq * L, 256)
    return out.reshape(nseq * L, 256)


def _mixout_kernel(x_ref, ya_ref, yb_ref, c1o, c1l, c2o, c2l, c3o, c3l, yd_ref,
                   gg_ref, wo_ref, gx_ref, wq_ref, xo_ref, qx_ref, *scr):
    tm = x_ref.shape[0]

    def gn(y, gi):
        return _rms_rows(y.astype(F32), gg_ref[gi:gi + 1, :]).astype(BF16)

    def token_order(ref, buf):
        d = ref.shape[1]
        for hb in range(2):
            for r in range(d):
                buf[hb, pl.ds(r, tm // d, stride=d), :] = ref[0, r, :, hb * LANES:(hb + 1) * LANES].astype(F32)
        return jnp.concatenate([buf[0], buf[1]], axis=-1)

    o1, l1 = c1o[...].astype(F32), c1l[...]
    o2, l2 = token_order(c2o, scr[0]), token_order(c2l, scr[1])
    o3, l3 = token_order(c3o, scr[2]), token_order(c3l, scr[3])
    mx = jnp.maximum(jnp.maximum(l1, l2), l3)
    e1, e2, e3 = jnp.exp(l1 - mx), jnp.exp(l2 - mx), jnp.exp(l3 - mx)
    yc = (e1 * o1 + e2 * o2 + e3 * o3) / (e1 + e2 + e3)
    ycat = jnp.concatenate([gn(ya_ref[...], 0), gn(yb_ref[...], 1), gn(yc, 2), gn(yd_ref[...], 3)], axis=-1)
    xn = x_ref[...] + jnp.dot(ycat, wo_ref[...], preferred_element_type=F32)
    xo_ref[...] = xn
    hq = _rms_rows(xn, gx_ref[...]).astype(BF16)
    qx_ref[...] = (jnp.dot(hq, wq_ref[...], preferred_element_type=F32) * (X_HEAD_DIM ** -0.5)).astype(BF16)


def _mixout(x, ya, yb, c1, c2, c3, yd, gg, wo, gx, wq, T):
    n = x.shape[0]
    B = n // T
    tm = TOKEN_TILE
    tpb = T // tm
    row = lambda wd: pl.BlockSpec((tm, wd), lambda i: (i, 0))
    slab = lambda d: pl.BlockSpec((1, d, tm // d, 256), lambda i: (i // tpb, 0, i % tpb, 0))
    d2, d3 = C_BRANCHES[1][1], C_BRANCHES[2][1]
    return pl.pallas_call(
        _mixout_kernel,
        grid=(n // tm,),
        in_specs=[row(D_MODEL)] + [row(256)] * 4 + [slab(d2)] * 2 + [slab(d3)] * 2 + [row(256)] + [
            _const_spec((4, GROUP_WIDTH)),
            _resident_spec((D_MODEL, D_MODEL)),
            _const_spec((1, D_MODEL)),
            _resident_spec((D_MODEL, D_MODEL)),
        ],
        out_specs=[row(D_MODEL), row(D_MODEL)],
        out_shape=[jax.ShapeDtypeStruct((n, D_MODEL), F32), jax.ShapeDtypeStruct((n, D_MODEL), BF16)],
        scratch_shapes=[pltpu.VMEM((2, tm, LANES), F32)] * 4,
        compiler_params=_cparams("parallel"),
        name="mixout",
    )(x, ya, yb, c1[0], c1[1], *[z.reshape(B, d2, T // d2, 256) for z in c2],
      *[z.reshape(B, d3, T // d3, 256) for z in c3], yd, gg, wo, gx, wq)


def _memkv_kernel(m_ref, g_ref, w_ref, kv_ref):
    h = _rms_rows(m_ref[...], g_ref[...]).astype(BF16)
    kv_ref[...] = jnp.dot(h, w_ref[...], preferred_element_type=F32).astype(BF16)


def _memkv(mem, g, w):
    n = mem.shape[0]
    return pl.pallas_call(
        _memkv_kernel,
        grid=(n // MEM_LEN,),
        in_specs=[
            pl.BlockSpec((MEM_LEN, D_MODEL), lambda i: (i, 0)),
            _const_spec((1, D_MODEL)),
            _resident_spec((D_MODEL, 2 * D_MODEL)),
        ],
        out_specs=pl.BlockSpec((MEM_LEN, 2 * D_MODEL), lambda i: (i, 0)),
        out_shape=jax.ShapeDtypeStruct((n, 2 * D_MODEL), BF16),
        compiler_params=_cparams("parallel"),
        name="memkv",
    )(mem, g, w)


def _cross_kernel(qx_ref, kv_ref, x_ref, wo_ref, xo_ref):
    outs = []
    for h in range(X_HEADS):
        cols = slice(h * X_HEAD_DIM, (h + 1) * X_HEAD_DIM)
        vcols = slice(D_MODEL + h * X_HEAD_DIM, D_MODEL + (h + 1) * X_HEAD_DIM)
        s = lax.dot_general(qx_ref[:, cols], kv_ref[:, cols], NT_DIMS, preferred_element_type=F32)
        m = jnp.max(s, axis=-1, keepdims=True)
        p = jnp.exp(s - m)
        l = jnp.sum(p, axis=-1, keepdims=True)
        o = jnp.dot(p.astype(BF16), kv_ref[:, vcols], preferred_element_type=F32) / l
        outs.append(o.astype(BF16))
    o = jnp.concatenate(outs, axis=-1)
    xo_ref[...] = x_ref[...] + jnp.dot(o, wo_ref[...], preferred_element_type=F32)


def _cross(qx, kv, x, wo, T):
    n = x.shape[0]
    tm = TOKEN_TILE
    tpb = T // tm
    row = lambda wd: pl.BlockSpec((tm, wd), lambda i: (i, 0))
    return pl.pallas_call(
        _cross_kernel,
        grid=(n // tm,),
        in_specs=[
            row(D_MODEL),
            pl.BlockSpec((MEM_LEN, 2 * D_MODEL), lambda i: (i // tpb, 0)),
            row(D_MODEL),
            _resident_spec((D_MODEL, D_MODEL)),
        ],
        out_specs=row(D_MODEL),
        out_shape=jax.ShapeDtypeStruct((n, D_MODEL), F32),
        compiler_params=_cparams("parallel"),
        name="cross",
    )(qx, kv, x, wo)


HALO = 16


def _ffn_kernel(xp_ref, x_ref, xn_ref, g_ref, wu_ref, cw_ref, cb_ref, wd_ref, gf_ref, o_ref, *, tpb, final):
    i = pl.program_id(0)
    tm = x_ref.shape[0]
    x = x_ref[...]
    xe = jnp.concatenate([xp_ref[...], x, xn_ref[...]], axis=0)
    he = _rms_rows(xe, g_ref[...]).astype(BF16)
    hc = he[HALO:HALO + tm]
    rows = lax.broadcasted_iota(jnp.int32, (tm + 2 * HALO, 1), 0)
    seq_first = (i % tpb) == 0
    seq_last = (i % tpb) == tpb - 1
    keep = jnp.logical_and(jnp.logical_or(rows >= HALO, jnp.logical_not(seq_first)),
                           jnp.logical_or(rows < HALO + tm, jnp.logical_not(seq_last)))
    keep = keep.astype(F32)
    acc = x
    for c0, c1 in FF_CHUNKS:
        gc = slice(c0, c1)
        uc = slice(D_FF + c0, D_FF + c1)
        ge = jnp.dot(he, wu_ref[:, gc], preferred_element_type=F32) * keep
        val = jnp.dot(hc, wu_ref[:, uc], preferred_element_type=F32)
        gate = (ge[HALO - 1:HALO - 1 + tm] * cw_ref[0:1, gc] + ge[HALO:HALO + tm] * cw_ref[1:2, gc]
                + ge[HALO + 1:HALO + 1 + tm] * cw_ref[2:3, gc] + cb_ref[:, gc])
        act = 0.5 * gate * (1.0 + lax.erf(gate * (2.0 ** -0.5))) * val
        acc = acc + jnp.dot(act.astype(BF16), wd_ref[gc, :], preferred_element_type=F32)
    if final:
        acc = _rms_rows(acc, gf_ref[...])
    o_ref[...] = acc


def _ffn(x, g, wu, cw, cb, wd, gf, T, final):
    n = x.shape[0]
    tm = FFN_TILE
    tpb = T // tm
    hb = tm // HALO
    nhb = n // HALO
    return pl.pallas_call(
        functools.partial(_ffn_kernel, tpb=tpb, final=final),
        grid=(n // tm,),
        in_specs=[
            pl.BlockSpec((HALO, D_MODEL), lambda i: (jnp.maximum(i * hb - 1, 0), 0)),
            pl.BlockSpec((tm, D_MODEL), lambda i: (i, 0)),
            pl.BlockSpec((HALO, D_MODEL), lambda i: (jnp.minimum((i + 1) * hb, nhb - 1), 0)),
            _const_spec((1, D_MODEL)),
            _resident_spec((D_MODEL, 2 * D_FF)),
            _const_spec((3, D_FF)),
            _const_spec((1, D_FF)),
            _resident_spec((D_FF, D_MODEL)),
            _const_spec((1, D_MODEL)),
        ],
        out_specs=pl.BlockSpec((tm, D_MODEL), lambda i: (i, 0)),
        out_shape=jax.ShapeDtypeStruct((n, D_MODEL), F32),
        compiler_params=_cparams("parallel"),
        name="ffn",
    )(x, x, x, g, wu, cw, cb, wd, gf)


def _rope_tables(T):
    t = jnp.arange(T)

    def cos_sin(pos, half):
        freqs = ROPE_THETA ** (-jnp.arange(half, dtype=F32) / half)
        ang = pos.astype(F32)[:, None] * freqs[None, :]
        return jnp.cos(ang), jnp.sin(ang)

    def lanes(c, s):
        return jnp.tile(c, (1, 4)), jnp.concatenate([-jnp.tile(s, (1, 2)), jnp.tile(s, (1, 2))], axis=-1)

    lin = lanes(*cos_sin(t, HEAD_DIM // 2))
    cr, sr = cos_sin(t // GRID_W, HEAD_DIM // 4)
    cc, sc = cos_sin(t % GRID_W, HEAD_DIM // 4)
    ax = lanes(jnp.concatenate([cr, cc], -1), jnp.concatenate([sr, sc], -1))
    return jnp.concatenate([*lin, *ax], axis=-1)


def _na_bias_table(rpb):
    c = np.arange(GRID_W)
    qstart = np.clip(c - NA_COLS // 2, 0, GRID_W - NA_COLS)
    rel = c[None, :] - qstart[:, None]
    valid = (rel >= 0) & (rel < NA_COLS)
    dc = np.clip(c[None, :] - c[:, None], -(NA_COLS - 1), NA_COLS - 1) + (NA_COLS - 1)
    e = np.arange(NA_ROWS)
    m = np.arange(NA_ROWS)
    dr = m[None, :] - e[:, None] + (NA_ROWS - 1)
    onehot = np.zeros((2 * NA_COLS - 1, GRID_W * GRID_W), np.float32)
    onehot[dc.reshape(-1), np.arange(GRID_W * GRID_W)] = 1.0
    rows = rpb[:, dr].reshape(-1, 2 * NA_COLS - 1)
    tbl = jnp.dot(rows, onehot, precision=lax.Precision.HIGHEST)
    tbl = tbl.reshape(rpb.shape[0], NA_ROWS, NA_ROWS, GRID_W, GRID_W)
    tbl = jnp.where(valid[None, None, None], tbl, NEG)
    tbl = tbl.transpose(1, 0, 3, 2, 4)
    return tbl.reshape(NA_ROWS, 2, 2 * GRID_W, NA_ROWS * GRID_W).astype(F32)


def _rotary_order(width, axial):
    idx = np.arange(width)
    nb = width // LANES
    if axial:
        idx = np.moveaxis(idx.reshape(nb, 2, 2, 2, HEAD_DIM // 4), -2, -4)
    else:
        idx = np.swapaxes(idx.reshape(nb, 2, 2, HEAD_DIM // 2), -2, -3)
    return idx.reshape(width)


def _gqa_order():
    return np.concatenate([np.arange(HEAD_DIM) + HEAD_DIM * hd for hd in GQA_HEAD_ORDER])


def _in_column_order():
    gqa = _gqa_order()
    return np.concatenate([
        gqa[_rotary_order(256, True)], 256 + _rotary_order(128, True), np.arange(384, 512),
        512 + gqa[_rotary_order(256, False)], 768 + _rotary_order(128, False), np.arange(896, 1024),
        1024 + _rotary_order(512, False), np.arange(1536, IN_WIDTH)])


def _trunk(x, mem, p):
    B, T, _ = x.shape
    n = B * T
    x = x.reshape(n, D_MODEL)
    tab = _rope_tables(T)
    head_of_lane = (np.arange(LANES) % HEAD_DIM) // (HEAD_DIM // 2)
    bd = jnp.asarray(head_of_lane[:, None] == head_of_lane[None, :], BF16)
    mem2 = mem.reshape(B * MEM_LEN, D_MODEL)
    for l in range(DEPTH):
        qa, ka, va, qb, kb, vb, qc, kc, vc, qc4, kc4, vc4, qc16, kc16, vc16, qd, kd, vd = _proj(
            x, p["norm_mix_g"][l], p["w_in"][l], tab, p["gq"][l], p["gk"][l], bd, T)
        ya = _attn_a(qa, ka, va, B, T)
        yb = _band(qb, kb, vb, nseq=B, L=T, S=BAND_BLOCK, W=BAND_BLOCK + 2 * B_RADIUS, R=B_RADIUS,
                   halo=B_RADIUS, mode="sink", extra=p["sink"][l])
        cs = []
        for (window, d), qkv in zip(C_BRANCHES, ((qc, kc, vc), (qc4, kc4, vc4), (qc16, kc16, vc16))):
            r = window // (2 * d)
            blk = min(C_BLOCK, T // d)
            qs, ks, vs = (z.reshape(n, 256) for z in qkv)
            cs.append(_band(qs, ks, vs, nseq=B * d, L=T // d, S=blk, W=blk + 2 * r, R=r, halo=r, mode="lse"))
        yd = _band(qd, kd, vd, nseq=B, L=T, S=GRID_W, W=NA_ROWS * GRID_W, R=0, halo=(NA_ROWS // 2) * GRID_W,
                   mode="bias", extra=p["na_bias"][l])
        x, qx = _mixout(x, ya, yb, cs[0], cs[1], cs[2], yd, p["grp_g"][l], p["w_out"][l],
                        p["norm_x_g"][l], p["w_xq"][l], T)
        kv = _memkv(mem2, p["norm_mem_g"][l], p["w_xkv"][l])
        x = _cross(qx, kv, x, p["w_xo"][l], T)
        x = _ffn(x, p["norm_ffn_g"][l], p["w_up"][l], p["conv_w"][l], p["conv_b"][l], p["w_down"][l],
                 p["final_g"], T, final=(l == DEPTH - 1))
    return x.reshape(B, T, D_MODEL)


def _prepare(norm_mix_g, w_in, a_q_norm_g, a_k_norm_g, b_sink, d_rpb, grp_norm_g, w_out, norm_x_g,
             norm_mem_g, w_xq, w_xkv, w_xo, norm_ffn_g, w_up, conv_w, conv_b, w_down, final_norm_g):
    layers = range(norm_mix_g.shape[0])

    gqa = _gqa_order()
    in_cols = _in_column_order()
    out_rows = np.concatenate([gqa, GROUP_WIDTH + gqa, np.arange(2 * GROUP_WIDTH, D_MODEL)])
    qk_lanes = _rotary_order(LANES, axial=True)

    def grp_layer(g):
        return jnp.concatenate([g[0:2][:, gqa], g[2:4]], axis=0)

    return {
        "norm_mix_g": [norm_mix_g[l][None, :] for l in layers],
        "w_in": [w_in[l][:, in_cols].astype(BF16) for l in layers],
        "gq": [jnp.tile(a_q_norm_g[l], 2)[qk_lanes][None, :] * (HEAD_DIM ** -0.5 * LOG2E) for l in layers],
        "gk": [jnp.tile(a_k_norm_g[l], 2)[qk_lanes][None, :] for l in layers],
        "sink": [b_sink[l][np.array(GQA_HEAD_ORDER)].reshape(2, 2) * LOG2E for l in layers],
        "na_bias": [_na_bias_table(d_rpb[l] * LOG2E) for l in layers],
        "grp_g": [grp_layer(grp_norm_g[l]) for l in layers],
        "w_out": [w_out[l][out_rows].astype(BF16) for l in layers],
        "norm_x_g": [norm_x_g[l][None, :] for l in layers],
        "norm_mem_g": [norm_mem_g[l][None, :] for l in layers],
        "w_xq": [w_xq[l].astype(BF16) for l in layers],
        "w_xkv": [w_xkv[l].astype(BF16) for l in layers],
        "w_xo": [w_xo[l].astype(BF16) for l in layers],
        "norm_ffn_g": [norm_ffn_g[l][None, :] for l in layers],
        "w_up": [w_up[l].astype(BF16) for l in layers],
        "conv_w": [conv_w[l] for l in layers],
        "conv_b": [conv_b[l][None, :] for l in layers],
        "w_down": [w_down[l].astype(BF16) for l in layers],
        "final_g": final_norm_g[None, :],
    }


def kernel(x_prompt, x_sample, mem_prompt, mem_sample, norm_mix_g, w_in, a_q_norm_g, a_k_norm_g, b_sink,
           d_rpb, grp_norm_g, w_out, norm_x_g, norm_mem_g, w_xq, w_xkv, w_xo, norm_ffn_g, w_up, conv_w,
           conv_b, w_down, final_norm_g):
    p = _prepare(norm_mix_g, w_in, a_q_norm_g, a_k_norm_g, b_sink, d_rpb, grp_norm_g, w_out, norm_x_g,
                 norm_mem_g, w_xq, w_xkv, w_xo, norm_ffn_g, w_up, conv_w, conv_b, w_down, final_norm_g)
    return _trunk(x_prompt, mem_prompt, p), _trunk(x_sample, mem_sample, p)
```
